```python
import math
import jax
import jax.numpy as jnp
from jax import lax
import numpy as np

D_MODEL = 2048
BATCH = 8
SEQ = 2048
DEPTH = 2
DEC_BATCH = 32
DEC_SEQ = 1
PAST_LEN = 8192
PAGE_SIZE = 128

HEAD_DIM = 128
N_HEADS = D_MODEL // HEAD_DIM
H_FOX = (3 * N_HEADS) // 8
H_MOBA = (3 * N_HEADS) // 8
H_DIFF = N_HEADS - H_FOX - H_MOBA
DIFF_QK_DIM = HEAD_DIM // 2
N_IN = 3 * D_MODEL + H_FOX
Q_BLOCK = 128
MOBA_BLOCK = 256
MOBA_TOPK = 3
MOBA_Q_CHUNK = 64
N_BUCKETS = 32
MAX_DISTANCE = 128
N_BIAS_HEADS = H_MOBA + H_DIFF
D_FF = 5632
N_EXPERTS = 8
TOP_K = 2
D_FF_EXPERT = 5632
N_DENSE = (DEPTH + 1) // 2
N_MOE = DEPTH // 2
ALPHA = (2.0 * DEPTH) ** 0.25
BETA = (8.0 * DEPTH) ** -0.25
LN_EPS = 1e-5
FORGET_BIAS = 2.0

kernel_name = 'hybrid_fox_moba_diff_decoder_step'

F32 = jnp.float32


def layer_norm(x, g, b):
    xf = x.astype(F32)
    xc = xf - jnp.mean(xf, -1, keepdims=True)
    var = jnp.mean(xc * xc, -1, keepdims=True)
    return (xc * lax.rsqrt(var + LN_EPS) * g.astype(F32) + b.astype(F32)).astype(x.dtype)


def rms_norm(x, g):
    xf = x.astype(F32)
    return xf * lax.rsqrt(jnp.mean(xf * xf, -1, keepdims=True) + LN_EPS) * g.astype(F32)


def t5_bucket(dist):
    max_exact = N_BUCKETS // 2
    d = jnp.maximum(dist, 0)
    ratio = jnp.log(jnp.maximum(d, 1).astype(F32) / max_exact) / math.log(MAX_DISTANCE / max_exact)
    large = jnp.minimum(max_exact + (ratio * (N_BUCKETS - max_exact)).astype(jnp.int32), N_BUCKETS - 1)
    return jnp.where(d < max_exact, d, large)


def project(x, w_in_l, b_f_l):
    b, t, _ = x.shape
    p = jnp.einsum('btd,dn->btn', x, w_in_l)
    q = p[..., :D_MODEL].reshape(b, t, N_HEADS, HEAD_DIM)
    k = p[..., D_MODEL:2 * D_MODEL].reshape(b, t, N_HEADS, HEAD_DIM)
    v = p[..., 2 * D_MODEL:3 * D_MODEL].reshape(b, t, N_HEADS, HEAD_DIM)
    logf = jax.nn.log_sigmoid((p[..., 3 * D_MODEL:] + b_f_l).astype(F32))
    return q, k, v, logf


def fox_scores(q, k, fq, fk):
    s = jnp.einsum('bqhd,bkhd->bhqk', q, k).astype(F32) * (HEAD_DIM ** -0.5)
    s = s + jnp.swapaxes(fq, 1, 2)[..., :, None] - jnp.swapaxes(fk, 1, 2)[..., None, :]
    return s[:, :, None]


def diff_scores(q, k, qpos, kpos, rel_bias):
    b, tq = q.shape[:2]
    kl = k.shape[1]
    qm = q.reshape(b, tq, H_DIFF, 2, DIFF_QK_DIM)
    km = k.reshape(b, kl, H_DIFF, 2, DIFF_QK_DIM)
    s = jnp.einsum('bqhmd,bkhmd->bhmqk', qm, km).astype(F32) * (DIFF_QK_DIM ** -0.5)
    bias = rel_bias[:, H_MOBA:][t5_bucket(qpos[:, None] - kpos[None, :])]
    return s + jnp.transpose(bias, (2, 0, 1)).astype(F32)[None, :, None]


def diff_lambda(l, lam_q1, lam_k1, lam_q2, lam_k2):
    lam_init = 0.8 - 0.6 * math.exp(-0.3 * l)
    lam = (jnp.exp(jnp.sum(lam_q1[l] * lam_k1[l]).astype(F32))
           - jnp.exp(jnp.sum(lam_q2[l] * lam_k2[l]).astype(F32)) + lam_init)
    return lam, lam_init


def diff_finish(o_maps, lam, lam_init, g):
    o = o_maps[..., 0, :].astype(F32) - lam * o_maps[..., 1, :].astype(F32)
    return rms_norm(o, g) * (1.0 - lam_init)


def dense_causal_prompt(q_f, k_f, v_f, f_cum, q_d, k_d, v_d, rel_bias):
    b, t = q_f.shape[:2]
    kpos = jnp.arange(t)

    def block(i):
        q0 = i * Q_BLOCK
        qpos = q0 + jnp.arange(Q_BLOCK)
        causal = kpos[None, :] <= qpos[:, None]
        qf = lax.dynamic_slice_in_dim(q_f, q0, Q_BLOCK, axis=1)
        fq = lax.dynamic_slice_in_dim(f_cum, q0, Q_BLOCK, axis=1)
        pf = jax.nn.softmax(jnp.where(causal, fox_scores(qf, k_f, fq, f_cum), -jnp.inf), axis=-1)
        of = jnp.einsum('bhmqk,bkhd->bqhmd', pf.astype(v_f.dtype), v_f)[:, :, :, 0]
        qd = lax.dynamic_slice_in_dim(q_d, q0, Q_BLOCK, axis=1)
        pd = jax.nn.softmax(jnp.where(causal, diff_scores(qd, k_d, qpos, kpos, rel_bias), -jnp.inf), axis=-1)
        od = jnp.einsum('bhmqk,bkhd->bqhmd', pd.astype(v_d.dtype), v_d)
        return of, od

    of, od = lax.map(block, jnp.arange(t // Q_BLOCK))
    of = jnp.swapaxes(of, 0, 1).reshape(b, t, H_FOX, HEAD_DIM)
    od = jnp.swapaxes(od, 0, 1).reshape(b, t, H_DIFF, 2, HEAD_DIM)
    return of, od


def online_init(s, v):
    m = jnp.max(s, -1)
    p = jnp.exp(s - m[..., None])
    return m, jnp.sum(p, -1), jnp.einsum('bhmqk,bkhd->bhmqd', p, v.astype(F32))


def online_update(state, s, v):
    m, lsum, acc = state
    m_new = jnp.maximum(m, jnp.max(s, -1))
    corr = jnp.exp(m - m_new)
    p = jnp.exp(s - m_new[..., None])
    return (m_new, lsum * corr + jnp.sum(p, -1),
            acc * corr[..., None] + jnp.einsum('bhmqk,bkhd->bhmqd', p, v.astype(F32)))


def online_finish(state):
    _, lsum, acc = state
    return jnp.transpose(acc / lsum[..., None], (0, 3, 1, 2, 4))


def dense_causal_sample(l, q_f, k_f, v_f, logf_new, q_d, k_d, v_d, cache_k, cache_v, cache_logf, page_table, rel_bias):
    b, tq = q_f.shape[:2]
    n_pages = page_table.shape[1]
    past = n_pages * PAGE_SIZE
    qpos = past + jnp.arange(tq)
    causal = qpos[None, :] <= qpos[:, None]
    fq = jnp.cumsum(logf_new, axis=1)
    f_past = jnp.cumsum(cache_logf[l, page_table].reshape(b, past, H_FOX).astype(F32), axis=1)
    f_past = f_past - f_past[:, -1:]
    st_f = online_init(jnp.where(causal, fox_scores(q_f, k_f, fq, fq), -jnp.inf), v_f)
    st_d = online_init(jnp.where(causal, diff_scores(q_d, k_d, qpos, qpos, rel_bias), -jnp.inf), v_d)
    heads = jnp.concatenate([jnp.arange(H_FOX), jnp.arange(H_FOX + H_MOBA, N_HEADS)])
    rows = jnp.arange(PAGE_SIZE)

    def step(carry, xs):
        sf, sd = carry
        p, fk = xs
        phys = page_table[:, p]
        kp = cache_k[l, phys[:, None, None], rows[None, :, None], heads[None, None, :]]
        vp = cache_v[l, phys[:, None, None], rows[None, :, None], heads[None, None, :]]
        kpos = p * PAGE_SIZE + rows
        sf = online_update(sf, fox_scores(q_f, kp[:, :, :H_FOX], fq, fk), vp[:, :, :H_FOX])
        sd = online_update(sd, diff_scores(q_d, kp[:, :, H_FOX:], qpos, kpos, rel_bias), vp[:, :, H_FOX:])
        return (sf, sd), None

    fk_pages = jnp.swapaxes(f_past.reshape(b, n_pages, PAGE_SIZE, H_FOX), 0, 1)
    (st_f, st_d), _ = lax.scan(step, (st_f, st_d), (jnp.arange(n_pages), fk_pages))
    return online_finish(st_f)[:, :, :, 0], online_finish(st_d)


def moba_attend(q, q0, block_means, fetch, bias_moba):
    b, t = q.shape[:2]
    nb = block_means.shape[1]
    n_sel = min(MOBA_TOPK, nb)
    c = math.gcd(t, MOBA_Q_CHUNK)
    nc = t // c
    hidx = jnp.arange(H_MOBA)[None, :, None, None]
    offs = jnp.arange(MOBA_BLOCK)

    def item(args):
        qc, bi, ci = args
        qpos = q0 + ci * c + jnp.arange(c)
        n_past = qpos // MOBA_BLOCK
        gate = jnp.einsum('chd,nhd->chn', qc.astype(F32), block_means[bi])
        gate = jnp.where(jnp.arange(nb)[None, None, :] < n_past[:, None, None], gate, -jnp.inf)
        _, sel = lax.top_k(gate, n_sel)
        own = jnp.broadcast_to(n_past[:, None, None], (c, H_MOBA, 1)).astype(sel.dtype)
        blocks = jnp.concatenate([sel, own], -1)
        keep = jnp.concatenate([
            jnp.broadcast_to(jnp.arange(n_sel)[None, None, :] < n_past[:, None, None], (c, H_MOBA, n_sel)),
            jnp.ones((c, H_MOBA, 1), bool)], -1)
        kpos = blocks[..., None] * MOBA_BLOCK + offs
        valid = keep[..., None] & (kpos <= qpos[:, None, None, None])
        kk, vv = fetch(bi, kpos, hidx)
        s = jnp.einsum('chd,chskd->chsk', qc, kk).astype(F32) * (HEAD_DIM ** -0.5)
        s = s + bias_moba[t5_bucket(qpos[:, None, None, None] - kpos), hidx].astype(F32)
        s = jnp.where(valid, s, -jnp.inf).reshape(c, H_MOBA, (n_sel + 1) * MOBA_BLOCK)
        p = jax.nn.softmax(s, -1).reshape(c, H_MOBA, n_sel + 1, MOBA_BLOCK)
        return jnp.einsum('chsk,chskd->chd', p.astype(vv.dtype), vv)

    qi = q.reshape(b * nc, c, H_MOBA, HEAD_DIM)
    bi = jnp.repeat(jnp.arange(b), nc)
    ci = jnp.tile(jnp.arange(nc), b)
    return lax.map(item, (qi, bi, ci)).reshape(b, t, H_MOBA, HEAD_DIM)


def block_means_prompt(k_m):
    b, t = k_m.shape[:2]
    nb = max((t - 1) // MOBA_BLOCK, 1)
    need = nb * MOBA_BLOCK
    kk = k_m[:, :need] if t >= need else jnp.pad(k_m, ((0, 0), (0, need - t), (0, 0), (0, 0)))
    return jnp.mean(kk.reshape(b, nb, MOBA_BLOCK, H_MOBA, HEAD_DIM).astype(F32), axis=2)


def gather_moba_rows(cache, new, l, page_table, bi, kpos, hidx):
    n_pages = page_table.shape[1]
    past = n_pages * PAGE_SIZE
    phys = page_table[bi, jnp.minimum(kpos // PAGE_SIZE, n_pages - 1)]
    from_cache = cache[l, phys, kpos % PAGE_SIZE, H_FOX + hidx]
    from_new = new[bi, jnp.clip(kpos - past, 0, new.shape[1] - 1), hidx]
    return jnp.where((kpos < past)[..., None], from_cache, from_new)


def block_means_sample(cache_k, k_m, l, page_table):
    b, t = k_m.shape[:2]
    past = page_table.shape[1] * PAGE_SIZE
    nb = max((past + t - 1) // MOBA_BLOCK, 1)
    bi = jnp.arange(b)[:, None, None]
    hidx = jnp.arange(H_MOBA)[None, None, :]
    offs = jnp.arange(MOBA_BLOCK)

    def one(j):
        kpos = (j * MOBA_BLOCK + offs)[None, :, None]
        return jnp.mean(gather_moba_rows(cache_k, k_m, l, page_table, bi, kpos, hidx).astype(F32), axis=1)

    return jnp.swapaxes(lax.map(one, jnp.arange(nb)), 0, 1)


def mixer_prompt(x, l, w_in, b_forget, w_out, lam_q1, lam_k1, lam_q2, lam_k2, subln_g, rel_bias):
    b, t, _ = x.shape
    h0, h1 = H_FOX, H_FOX + H_MOBA
    q, k, v, logf = project(x, w_in[l], b_forget[l])
    of, od_maps = dense_causal_prompt(q[:, :, :h0], k[:, :, :h0], v[:, :, :h0], jnp.cumsum(logf, axis=1),
                                      q[:, :, h1:], k[:, :, h1:], v[:, :, h1:], rel_bias)
    k_m, v_m = k[:, :, h0:h1], v[:, :, h0:h1]

    def fetch(bi, kpos, hidx):
        kc = jnp.minimum(kpos, t - 1)
        return k_m[bi, kc, hidx], v_m[bi, kc, hidx]

    om = moba_attend(q[:, :, h0:h1], 0, block_means_prompt(k_m), fetch, rel_bias[:, :H_MOBA])
    lam, lam_init = diff_lambda(l, lam_q1, lam_k1, lam_q2, lam_k2)
    od = diff_finish(od_maps, lam, lam_init, subln_g[l])
    o = jnp.concatenate([of.astype(x.dtype), om.astype(x.dtype), od.astype(x.dtype)], axis=2).reshape(b, t, D_MODEL)
    return jnp.einsum('btd,de->bte', o, w_out[l]), k, v, logf


def mixer_sample(x, l, cache_k, cache_v, cache_logf, page_table, w_in, b_forget, w_out,
                 lam_q1, lam_k1, lam_q2, lam_k2, subln_g, rel_bias):
    b, t, _ = x.shape
    h0, h1 = H_FOX, H_FOX + H_MOBA
    q, k, v, logf = project(x, w_in[l], b_forget[l])
    of, od_maps = dense_causal_sample(l, q[:, :, :h0], k[:, :, :h0], v[:, :, :h0], logf,
                                      q[:, :, h1:], k[:, :, h1:], v[:, :, h1:],
                                      cache_k, cache_v, cache_logf, page_table, rel_bias)
    k_m, v_m = k[:, :, h0:h1], v[:, :, h0:h1]

    def fetch(bi, kpos, hidx):
        return (gather_moba_rows(cache_k, k_m, l, page_table, bi, kpos, hidx),
                gather_moba_rows(cache_v, v_m, l, page_table, bi, kpos, hidx))

    past = page_table.shape[1] * PAGE_SIZE
    om = moba_attend(q[:, :, h0:h1], past, block_means_sample(cache_k, k_m, l, page_table), fetch, rel_bias[:, :H_MOBA])
    lam, lam_init = diff_lambda(l, lam_q1, lam_k1, lam_q2, lam_k2)
    od = diff_finish(od_maps, lam, lam_init, subln_g[l])
    o = jnp.concatenate([of.astype(x.dtype), om.astype(x.dtype), od.astype(x.dtype)], axis=2).reshape(b, t, D_MODEL)
    return jnp.einsum('btd,de->bte', o, w_out[l]), k, v, logf


def swiglu(x, wg, wu, wd):
    h = jax.nn.silu(jnp.einsum('btd,df->btf', x, wg)) * jnp.einsum('btd,df->btf', x, wu)
    return jnp.einsum('btf,fd->btd', h, wd)


def moe_swiglu(x, w_r, b_r, wg, wu, wd):
    logits = (jnp.einsum('btd,de->bte', x, w_r) + b_r).astype(F32)
    top_v, top_i = lax.top_k(logits, TOP_K)
    gates = jax.nn.softmax(top_v, -1)
    combine = jnp.sum(jax.nn.one_hot(top_i, N_EXPERTS, dtype=F32) * gates[..., None], axis=-2)
    y = jnp.zeros_like(x)
    for e in range(N_EXPERTS):
        y = y + combine[..., e:e + 1].astype(x.dtype) * swiglu(x, wg[e], wu[e], wd[e])
    return y


def channel_mixer(x, l, w_ffn_gate, w_ffn_up, w_ffn_down, w_router, b_router, w_exp_gate, w_exp_up, w_exp_down):
    i = l // 2
    if l % 2 == 0:
        return swiglu(x, w_ffn_gate[i], w_ffn_up[i], w_ffn_down[i])
    return moe_swiglu(x, w_router[i], b_router[i], w_exp_gate[i], w_exp_up[i], w_exp_down[i])


def to_pages(a):
    return a.reshape(a.shape[0], a.shape[1] // PAGE_SIZE, PAGE_SIZE, *a.shape[2:])


def setup_inputs(seed: int = 0) -> dict:
    key = jax.random.key(seed)
    ks = jax.random.split(key, 32)
    n_pages = PAST_LEN // PAGE_SIZE
    n_pool = (5 * DEC_BATCH * n_pages + 3) // 4

    def nrm(k, shape, scale):
        return jax.random.normal(k, shape, F32) * scale

    x_prompt = jax.random.normal(ks[0], (BATCH, SEQ, D_MODEL), F32)
    x_sample = jax.random.normal(ks[1], (DEC_BATCH, DEC_SEQ, D_MODEL), F32)
    cache_k = jax.random.normal(ks[2], (DEPTH, n_pool, PAGE_SIZE, N_HEADS, HEAD_DIM), F32)
    cache_v = jax.random.normal(ks[3], (DEPTH, n_pool, PAGE_SIZE, N_HEADS, HEAD_DIM), F32)
    cache_logf = jax.nn.log_sigmoid(FORGET_BIAS + jax.random.normal(ks[4], (DEPTH, n_pool, PAGE_SIZE, H_FOX), F32))
    page_table = jax.random.permutation(ks[5], n_pool)[:DEC_BATCH * n_pages].reshape(DEC_BATCH, n_pages).astype(jnp.int32)
    return {
        'x_prompt': x_prompt,
        'x_sample': x_sample,
        'cache_k': cache_k,
        'cache_v': cache_v,
        'cache_logf': cache_logf,
        'page_table': page_table,
        'w_in': nrm(ks[6], (DEPTH, D_MODEL, N_IN), D_MODEL ** -0.5),
        'b_forget': FORGET_BIAS + nrm(ks[7], (DEPTH, H_FOX), 0.1),
        'w_out': nrm(ks[8], (DEPTH, D_MODEL, D_MODEL), BETA * D_MODEL ** -0.5),
        'lam_q1': nrm(ks[9], (DEPTH, DIFF_QK_DIM), 0.1),
        'lam_k1': nrm(ks[10], (DEPTH, DIFF_QK_DIM), 0.1),
        'lam_q2': nrm(ks[11], (DEPTH, DIFF_QK_DIM), 0.1),
        'lam_k2': nrm(ks[12], (DEPTH, DIFF_QK_DIM), 0.1),
        'subln_g': 1.0 + nrm(ks[13], (DEPTH, HEAD_DIM), 0.01),
        'rel_bias': nrm(ks[14], (N_BUCKETS, N_BIAS_HEADS), 0.2),
        'ln1_g': 1.0 + nrm(ks[15], (DEPTH, D_MODEL), 0.01),
        'ln1_b': nrm(ks[16], (DEPTH, D_MODEL), 0.01),
        'ln2_g': 1.0 + nrm(ks[17], (DEPTH, D_MODEL), 0.01),
        'ln2_b': nrm(ks[18], (DEPTH, D_MODEL), 0.01),
        'w_ffn_gate': nrm(ks[19], (N_DENSE, D_MODEL, D_FF), D_MODEL ** -0.5),
        'w_ffn_up': nrm(ks[20], (N_DENSE, D_MODEL, D_FF), D_MODEL ** -0.5),
        'w_ffn_down': nrm(ks[21], (N_DENSE, D_FF, D_MODEL), BETA * D_FF ** -0.5),
        'w_router': nrm(ks[22], (N_MOE, D_MODEL, N_EXPERTS), D_MODEL ** -0.5),
        'b_router': nrm(ks[23], (N_MOE, N_EXPERTS), 0.01),
        'w_exp_gate': nrm(ks[24], (N_MOE, N_EXPERTS, D_MODEL, D_FF_EXPERT), D_MODEL ** -0.5),
        'w_exp_up': nrm(ks[25], (N_MOE, N_EXPERTS, D_MODEL, D_FF_EXPERT), D_MODEL ** -0.5),
        'w_exp_down': nrm(ks[26], (N_MOE, N_EXPERTS, D_FF_EXPERT, D_MODEL), BETA * D_FF_EXPERT ** -0.5),
    }


def reference(x_prompt, x_sample, cache_k, cache_v, cache_logf, page_table, w_in, b_forget, w_out,
              lam_q1, lam_k1, lam_q2, lam_k2, subln_g, rel_bias, ln1_g, ln1_b, ln2_g, ln2_b,
              w_ffn_gate, w_ffn_up, w_ffn_down, w_router, b_router, w_exp_gate, w_exp_up, w_exp_down):
    xp, xs = x_prompt, x_sample
    kp_all, vp_all, fp_all, ks_all, vs_all, fs_all = [], [], [], [], [], []
    for l in range(DEPTH):
        ap, kp, vp, fp = mixer_prompt(xp, l, w_in, b_forget, w_out, lam_q1, lam_k1, lam_q2, lam_k2, subln_g, rel_bias)
        asm, ksm, vsm, fsm = mixer_sample(xs, l, cache_k, cache_v, cache_logf, page_table, w_in, b_forget, w_out,
                                          lam_q1, lam_k1, lam_q2, lam_k2, subln_g, rel_bias)
        xp = layer_norm(ALPHA * xp + ap, ln1_g[l], ln1_b[l])
        xs = layer_norm(ALPHA * xs + asm, ln1_g[l], ln1_b[l])
        xp = layer_norm(ALPHA * xp + channel_mixer(xp, l, w_ffn_gate, w_ffn_up, w_ffn_down, w_router, b_router,
                                                   w_exp_gate, w_exp_up, w_exp_down), ln2_g[l], ln2_b[l])
        xs = layer_norm(ALPHA * xs + channel_mixer(xs, l, w_ffn_gate, w_ffn_up, w_ffn_down, w_router, b_router,
                                                   w_exp_gate, w_exp_up, w_exp_down), ln2_g[l], ln2_b[l])
        kp_all.append(to_pages(kp))
        vp_all.append(to_pages(vp))
        fp_all.append(to_pages(fp))
        ks_all.append(ksm)
        vs_all.append(vsm)
        fs_all.append(fsm)
    new_k_prompt = jnp.stack(kp_all)
    new_v_prompt = jnp.stack(vp_all)
    new_logf_prompt = jnp.stack(fp_all)
    new_k_sample = jnp.stack(ks_all)
    new_v_sample = jnp.stack(vs_all)
    new_logf_sample = jnp.stack(fs_all)
    return (xp, xs, new_k_prompt, new_v_prompt, new_logf_prompt, new_k_sample, new_v_sample, new_logf_sample)
```

```python
import functools
import math

import jax
import jax.numpy as jnp
from jax import lax
from jax.experimental import pallas as pl
from jax.experimental.pallas import tpu as pltpu

F32 = jnp.float32
BF16 = jnp.bfloat16
HIGHEST = lax.Precision.HIGHEST
NEG_INF = float("-inf")

HEAD_DIM = 128
N_HEADS = 16
D_MODEL = N_HEADS * HEAD_DIM
H_FOX = 6
H_MOBA = 6
H_DIFF = 4
DIFF_QK_DIM = HEAD_DIM // 2
N_QMAPS = H_FOX + H_MOBA + 2 * H_DIFF
MOBA_BLOCK = 256
MOBA_TOPK = 3
PAGE_SIZE = 128
N_BUCKETS = 32
MAX_DISTANCE = 128
N_EXPERTS = 8
TOP_K = 2
DEPTH = 2
ALPHA = (2.0 * DEPTH) ** 0.25
LN_EPS = 1e-5

LANES = 128
VMEM_LIMIT_BYTES = 56 * 1024 * 1024
ATT_TILE = MOBA_BLOCK
SAMPLE_ROWS = 2 * N_HEADS


def _cparams(n_axes):
    return pltpu.CompilerParams(
        dimension_semantics=("arbitrary",) * n_axes, vmem_limit_bytes=VMEM_LIMIT_BYTES)


def _t5_thresholds():
    max_exact = N_BUCKETS // 2

    def bucket(d):
        if d < max_exact:
            return d
        r = math.log(d / max_exact) / math.log(MAX_DISTANCE / max_exact)
        return min(max_exact + int(r * (N_BUCKETS - max_exact)), N_BUCKETS - 1)

    return [min(d for d in range(2 * MAX_DISTANCE) if bucket(d) >= b) for b in range(N_BUCKETS)]


T5_THRESHOLDS = _t5_thresholds()
T5_FAR = T5_THRESHOLDS[-1]


def _t5_bucket_static(d):
    b = 0
    for i, t in enumerate(T5_THRESHOLDS):
        if d >= t:
            b = i
    return b


def _mm_body(x_ref, w_ref, *o_refs):
    acc = jnp.dot(x_ref[...], w_ref[...], preferred_element_type=F32)
    for o in o_refs:
        o[...] = acc.astype(o.dtype)


def _matmul(x, w, out_dtypes, tm, tn, name):
    m, k = x.shape
    n = w.shape[1]
    tm, tn = min(tm, m), min(tn, n)
    return pl.pallas_call(
        _mm_body,
        grid=(m // tm, n // tn),
        in_specs=[pl.BlockSpec((tm, k), lambda i, j: (i, 0)),
                  pl.BlockSpec((k, tn), lambda i, j: (0, j))],
        out_specs=[pl.BlockSpec((tm, tn), lambda i, j: (i, j)) for _ in out_dtypes],
        out_shape=[jax.ShapeDtypeStruct((m, n), d) for d in out_dtypes],
        compiler_params=_cparams(2),
        name=name,
    )(x, w)


def _log_sigmoid(z):
    return jnp.minimum(z, 0.0) - jnp.log(1.0 + jnp.exp(-jnp.abs(z)))


def _logf_body(x_ref, w_ref, b_ref, lf_ref, cum_ref, carry, *, tt, cumulative):
    t = pl.program_id(1)
    z = jnp.dot(x_ref[...], w_ref[...], precision=HIGHEST, preferred_element_type=F32) + b_ref[...]
    lf = _log_sigmoid(z)
    lf_ref[...] = lf
    if cumulative:
        @pl.when(t == 0)
        def _():
            carry[...] = jnp.zeros_like(carry)

        row = lax.broadcasted_iota(jnp.int32, (tt, tt), 0)
        col = lax.broadcasted_iota(jnp.int32, (tt, tt), 1)
        tri = jnp.where(col <= row, 1.0, 0.0).astype(F32)
        cum = jnp.dot(tri, lf, precision=HIGHEST, preferred_element_type=F32) + carry[...]
        cum_ref[...] = cum
        carry[...] = cum[tt - 1:tt, :]
    else:
        cum_ref[...] = lf


def _log_forget(x32, w_f, b_f, n_seq, cumulative, name):
    m = x32.shape[0]
    t_len = m // n_seq
    tt = min(ATT_TILE, t_len) if cumulative else m
    n_seq_grid = n_seq if cumulative else 1
    nt = (m // n_seq_grid) // tt
    spec = pl.BlockSpec((tt, LANES), lambda b, t: (b * nt + t, 0))
    return pl.pallas_call(
        functools.partial(_logf_body, tt=tt, cumulative=cumulative),
        grid=(n_seq_grid, nt),
        in_specs=[pl.BlockSpec((tt, D_MODEL), lambda b, t: (b * nt + t, 0)),
                  pl.BlockSpec((D_MODEL, LANES), lambda b, t: (0, 0)),
                  pl.BlockSpec((1, LANES), lambda b, t: (0, 0))],
        out_specs=[spec, spec],
        out_shape=[jax.ShapeDtypeStruct((m, LANES), F32)] * 2,
        scratch_shapes=[pltpu.VMEM((1, LANES), F32)],
        compiler_params=_cparams(2),
        name=name,
    )(x32, w_f, b_f)


def _layer_norm_rows(y, g, b):
    mean = jnp.mean(y, axis=1, keepdims=True)
    yc = y - mean
    var = jnp.mean(yc * yc, axis=1, keepdims=True)
    return yc * lax.rsqrt(var + LN_EPS) * g + b


def _outproj_ln_body(of_ref, om_ref, od_ref, w_ref, x_ref, g_ref, b_ref, o32_ref, ob_ref):
    nf, nm = H_FOX * HEAD_DIM, (H_FOX + H_MOBA) * HEAD_DIM
    a = jnp.dot(of_ref[...], w_ref[0:nf, :], preferred_element_type=F32)
    a += jnp.dot(om_ref[...], w_ref[nf:nm, :], preferred_element_type=F32)
    a += jnp.dot(od_ref[...], w_ref[nm:D_MODEL, :], preferred_element_type=F32)
    y = _layer_norm_rows(ALPHA * x_ref[...] + a, g_ref[...], b_ref[...])
    o32_ref[...] = y
    ob_ref[...] = y.astype(BF16)


def _outproj_ln(o_fox, o_moba, o_diff, w_out, x32, g, b, tm, name):
    m = x32.shape[0]
    tm = min(tm, m)
    row = lambda i: (i, 0)
    fixed = lambda i: (0, 0)
    return pl.pallas_call(
        _outproj_ln_body,
        grid=(m // tm,),
        in_specs=[pl.BlockSpec((tm, o_fox.shape[1]), row),
                  pl.BlockSpec((tm, o_moba.shape[1]), row),
                  pl.BlockSpec((tm, o_diff.shape[1]), row),
                  pl.BlockSpec((D_MODEL, D_MODEL), fixed),
                  pl.BlockSpec((tm, D_MODEL), row),
                  pl.BlockSpec((1, D_MODEL), fixed),
                  pl.BlockSpec((1, D_MODEL), fixed)],
        out_specs=[pl.BlockSpec((tm, D_MODEL), row)] * 2,
        out_shape=[jax.ShapeDtypeStruct((m, D_MODEL), F32), jax.ShapeDtypeStruct((m, D_MODEL), BF16)],
        compiler_params=_cparams(1),
        name=name,
    )(o_fox, o_moba, o_diff, w_out, x32, g, b)


def _swiglu_up_body(x_ref, wg_ref, wu_ref, h_ref):
    x = x_ref[...]
    g = jnp.dot(x, wg_ref[...], preferred_element_type=F32)
    u = jnp.dot(x, wu_ref[...], preferred_element_type=F32)
    h_ref[...] = (g * (1.0 / (1.0 + jnp.exp(-g))) * u).astype(h_ref.dtype)


def _swiglu_up(xb, wg, wu, tm, tn, name):
    m, k = xb.shape
    f = wg.shape[1]
    tm, tn = min(tm, m), min(tn, f)
    return pl.pallas_call(
        _swiglu_up_body,
        grid=(m // tm, f // tn),
        in_specs=[pl.BlockSpec((tm, k), lambda i, j: (i, 0)),
                  pl.BlockSpec((k, tn), lambda i, j: (0, j)),
                  pl.BlockSpec((k, tn), lambda i, j: (0, j))],
        out_specs=pl.BlockSpec((tm, tn), lambda i, j: (i, j)),
        out_shape=jax.ShapeDtypeStruct((m, f), BF16),
        compiler_params=_cparams(2),
        name=name,
    )(xb, wg, wu)


def _down_ln_body(h_ref, w_ref, x_ref, g_ref, b_ref, o32_ref, ob_ref, acc):
    kk = pl.program_id(1)

    @pl.when(kk == 0)
    def _():
        acc[...] = jnp.zeros_like(acc)

    acc[...] += jnp.dot(h_ref[...], w_ref[...], preferred_element_type=F32)

    @pl.when(kk == pl.num_programs(1) - 1)
    def _():
        y = _layer_norm_rows(ALPHA * x_ref[...] + acc[...], g_ref[...], b_ref[...])
        o32_ref[...] = y
        ob_ref[...] = y.astype(BF16)


def _down_ln(h, wd, x32, g, b, tm, tk, name):
    m, f = h.shape
    tm, tk = min(tm, m), min(tk, f)
    row = lambda i, k: (i, 0)
    fixed = lambda i, k: (0, 0)
    return pl.pallas_call(
        _down_ln_body,
        grid=(m // tm, f // tk),
        in_specs=[pl.BlockSpec((tm, tk), lambda i, k: (i, k)),
                  pl.BlockSpec((tk, D_MODEL), lambda i, k: (k, 0)),
                  pl.BlockSpec((tm, D_MODEL), row),
                  pl.BlockSpec((1, D_MODEL), fixed),
                  pl.BlockSpec((1, D_MODEL), fixed)],
        out_specs=[pl.BlockSpec((tm, D_MODEL), row)] * 2,
        out_shape=[jax.ShapeDtypeStruct((m, D_MODEL), F32), jax.ShapeDtypeStruct((m, D_MODEL), BF16)],
        scratch_shapes=[pltpu.VMEM((tm, D_MODEL), F32)],
        compiler_params=_cparams(2),
        name=name,
    )(h, wd, x32, g, b)


def _online_softmax_step(s, v, m_sc, l_sc, acc_sc):
    m_prev = m_sc[...]
    m_new = jnp.maximum(m_prev, jnp.max(s, axis=1, keepdims=True))
    alpha = jnp.exp(m_prev - m_new)
    p = jnp.exp(s - m_new)
    l_sc[...] = alpha * l_sc[...] + jnp.sum(p, axis=1, keepdims=True)
    acc_sc[...] = alpha * acc_sc[...] + jnp.dot(p.astype(BF16), v, preferred_element_type=F32)
    m_sc[...] = m_new


def _init_softmax_state(m_sc, l_sc, acc_sc):
    m_sc[...] = jnp.full(m_sc.shape, NEG_INF, F32)
    l_sc[...] = jnp.zeros_like(l_sc)
    acc_sc[...] = jnp.zeros_like(acc_sc)


def _qk(q, k):
    return lax.dot_general(q, k, (((1,), (1,)), ((), ())), preferred_element_type=F32)


def _lane_column(x, idx):
    lane = lax.broadcasted_iota(jnp.int32, x.shape, 1)
    return jnp.sum(jnp.where(lane == idx, x, 0.0), axis=1, keepdims=True)


def _fox_prompt_body(q_ref, k_ref, v_ref, frow_ref, fcol_ref, o_ref, m_sc, l_sc, acc_sc, *, tq):
    h = pl.program_id(1)
    qi = pl.program_id(2)
    scale = HEAD_DIM ** -0.5
    q = q_ref[...]
    frow = _lane_column(frow_ref[...], h)
    _init_softmax_state(m_sc, l_sc, acc_sc)

    def step(c, diagonal):
        start = pl.multiple_of(c * tq, tq)
        s = _qk(q, k_ref[pl.ds(start, tq), :]) * scale
        s = s + (frow - fcol_ref[0, 0, :, pl.ds(start, tq)])
        if diagonal:
            row = lax.broadcasted_iota(jnp.int32, (tq, tq), 0)
            col = lax.broadcasted_iota(jnp.int32, (tq, tq), 1)
            s = jnp.where(col <= row, s, NEG_INF)
        _online_softmax_step(s, v_ref[pl.ds(start, tq), :], m_sc, l_sc, acc_sc)

    step(qi, True)

    def past(c, carry):
        step(c, False)
        return carry

    lax.fori_loop(0, qi, past, 0)
    o_ref[...] = (acc_sc[...] / l_sc[...]).astype(o_ref.dtype)


def _fox_prompt(q_maps, kb, vb, f_rows, f_cols, n_seq, name):
    m = q_maps.shape[0]
    t_len = m // n_seq
    tq = min(ATT_TILE, t_len)
    nq = t_len // tq
    return pl.pallas_call(
        functools.partial(_fox_prompt_body, tq=tq),
        grid=(n_seq, H_FOX, nq),
        in_specs=[pl.BlockSpec((tq, HEAD_DIM), lambda b, h, i: (b * nq + i, h)),
                  pl.BlockSpec((t_len, HEAD_DIM), lambda b, h, i: (b, h)),
                  pl.BlockSpec((t_len, HEAD_DIM), lambda b, h, i: (b, h)),
                  pl.BlockSpec((tq, LANES), lambda b, h, i: (b * nq + i, 0)),
                  pl.BlockSpec((1, 1, 1, t_len), lambda b, h, i: (b, h, 0, 0))],
        out_specs=pl.BlockSpec((tq, HEAD_DIM), lambda b, h, i: (b * nq + i, h)),
        out_shape=jax.ShapeDtypeStruct((m, H_FOX * HEAD_DIM), BF16),
        scratch_shapes=[pltpu.VMEM((tq, 1), F32), pltpu.VMEM((tq, 1), F32), pltpu.VMEM((tq, HEAD_DIM), F32)],
        compiler_params=_cparams(3),
        name=name,
    )(q_maps, kb, vb, f_rows, f_cols)


def _bias_tiles_body(rel_ref, o_ref, *, tq):
    h = pl.program_id(0)
    row = lax.broadcasted_iota(jnp.int32, (tq, tq), 0)
    col = lax.broadcasted_iota(jnp.int32, (tq, tq), 1)
    for kind in range(2):
        d = row - col + kind * tq
        bias = jnp.full((tq, tq), rel_ref[h, 0], F32)
        for bkt in range(1, N_BUCKETS):
            bias = jnp.where(d >= T5_THRESHOLDS[bkt], rel_ref[h, bkt], bias)
        if kind == 0:
            bias = jnp.where(d >= 0, bias, NEG_INF)
        o_ref[0, kind] = bias


def _bias_tiles(rel_t, tq):
    nh = rel_t.shape[0]
    return pl.pallas_call(
        functools.partial(_bias_tiles_body, tq=tq),
        grid=(nh,),
        in_specs=[pl.BlockSpec(memory_space=pltpu.SMEM)],
        out_specs=pl.BlockSpec((1, 2, tq, tq), lambda h: (h, 0, 0, 0)),
        out_shape=jax.ShapeDtypeStruct((nh, 2, tq, tq), F32),
        compiler_params=_cparams(1),
        name="t5_bias_tiles",
    )(rel_t)


def _block_means_body(k_ref, o_ref, *, n_blocks):
    k = k_ref[...]
    o_ref[...] = jnp.zeros_like(o_ref)
    km = jnp.mean(k.reshape(n_blocks, MOBA_BLOCK, HEAD_DIM), axis=1)
    o_ref[0, 0, 0:n_blocks, :] = km


def _block_means(k32, n_seq, name):
    m = k32.shape[0]
    t_len = m // n_seq
    n_blocks = t_len // MOBA_BLOCK
    return pl.pallas_call(
        functools.partial(_block_means_body, n_blocks=n_blocks),
        grid=(n_seq, H_MOBA),
        in_specs=[pl.BlockSpec((t_len, HEAD_DIM), lambda b, h: (b, H_FOX + h))],
        out_specs=pl.BlockSpec((1, 1, LANES, HEAD_DIM), lambda b, h: (b, h, 0, 0)),
        out_shape=jax.ShapeDtypeStruct((n_seq, H_MOBA, LANES, HEAD_DIM), F32),
        compiler_params=_cparams(2),
        name=name,
    )(k32)


def _top_blocks(gate, n_valid, n_candidates, n_sel):
    lane = lax.broadcasted_iota(jnp.int32, gate.shape, 1)
    rank = jnp.zeros(gate.shape, F32)
    for c2 in range(n_candidates):
        g2 = gate[:, c2:c2 + 1]
        beats = jnp.where(g2 > gate, 1.0, jnp.where(g2 == gate, jnp.where(lane > c2, 1.0, 0.0), 0.0))
        rank = rank + beats * jnp.where(c2 < n_valid, 1.0, 0.0)
    return jnp.where(rank < n_sel, jnp.where(lane < n_valid, 1.0, 0.0), 0.0)


def _moba_prompt_body(far_ref, q_ref, k_ref, v_ref, means_ref, bias_ref, o_ref, m_sc, l_sc, acc_sc,
                      *, tq, n_candidates, n_sel):
    h = pl.program_id(1)
    qi = pl.program_id(2)
    scale = HEAD_DIM ** -0.5
    q = q_ref[...]
    gate = lax.dot_general(q.astype(F32), means_ref[0, 0], (((1,), (1,)), ((), ())),
                           precision=HIGHEST, preferred_element_type=F32)
    sel = _top_blocks(gate, qi, n_candidates, n_sel)
    _init_softmax_state(m_sc, l_sc, acc_sc)

    def step(c, bias, selected):
        start = pl.multiple_of(c * tq, tq)
        s = _qk(q, k_ref[pl.ds(start, tq), :]) * scale + bias
        if selected:
            s = jnp.where(_lane_column(sel, c) > 0.0, s, NEG_INF)
        _online_softmax_step(s, v_ref[pl.ds(start, tq), :], m_sc, l_sc, acc_sc)

    step(qi, bias_ref[0, 0], False)

    @pl.when(qi > 0)
    def _():
        step(qi - 1, bias_ref[0, 1], True)

    far = far_ref[h]

    def past(c, carry):
        step(c, far, True)
        return carry

    lax.fori_loop(0, jnp.maximum(qi - 1, 0), past, 0)
    o_ref[...] = (acc_sc[...] / l_sc[...]).astype(o_ref.dtype)


def _moba_prompt(q_maps, kb, vb, means, bias_tiles, far, n_seq, name):
    m = q_maps.shape[0]
    t_len = m // n_seq
    tq = ATT_TILE
    nq = t_len // tq
    n_candidates = max((t_len - 1) // MOBA_BLOCK, 1)
    n_sel = min(MOBA_TOPK, n_candidates)
    grid_spec = pltpu.PrefetchScalarGridSpec(
        num_scalar_prefetch=1,
        grid=(n_seq, H_MOBA, nq),
        in_specs=[pl.BlockSpec((tq, HEAD_DIM), lambda b, h, i, far: (b * nq + i, H_FOX + h)),
                  pl.BlockSpec((t_len, HEAD_DIM), lambda b, h, i, far: (b, H_FOX + h)),
                  pl.BlockSpec((t_len, HEAD_DIM), lambda b, h, i, far: (b, H_FOX + h)),
                  pl.BlockSpec((1, 1, LANES, HEAD_DIM), lambda b, h, i, far: (b, h, 0, 0)),
                  pl.BlockSpec((1, 2, tq, tq), lambda b, h, i, far: (h, 0, 0, 0))],
        out_specs=pl.BlockSpec((tq, HEAD_DIM), lambda b, h, i, far: (b * nq + i, h)),
        scratch_shapes=[pltpu.VMEM((tq, 1), F32), pltpu.VMEM((tq, 1), F32), pltpu.VMEM((tq, HEAD_DIM), F32)],
    )
    return pl.pallas_call(
        functools.partial(_moba_prompt_body, tq=tq, n_candidates=n_candidates, n_sel=n_sel),
        grid_spec=grid_spec,
        out_shape=jax.ShapeDtypeStruct((m, H_MOBA * HEAD_DIM), BF16),
        compiler_params=_cparams(3),
        name=name,
    )(far, q_maps, kb, vb, means, bias_tiles)


def _diff_lambda(lam_ref, lam_init):
    r = lam_ref[...]
    s1 = jnp.sum(r[0:1, :] * r[1:2, :], axis=1, keepdims=True)
    s2 = jnp.sum(r[2:3, :] * r[3:4, :], axis=1, keepdims=True)
    return jnp.exp(s1) - jnp.exp(s2) + lam_init


def _diff_finish(o0, o1, lam, g, lam_init):
    o = o0 - lam * o1
    return o * lax.rsqrt(jnp.mean(o * o, axis=1, keepdims=True) + LN_EPS) * g * (1.0 - lam_init)


def _diff_prompt_body(far_ref, q_ref, k_ref, v_ref, bias_ref, lam_ref, g_ref, o_ref,
                      m0, l0, a0, m1, l1, a1, *, tq, lam_init):
    h = pl.program_id(1)
    qi = pl.program_id(2)
    scale = DIFF_QK_DIM ** -0.5
    q0 = q_ref[:, 0:HEAD_DIM]
    q1 = q_ref[:, HEAD_DIM:2 * HEAD_DIM]
    _init_softmax_state(m0, l0, a0)
    _init_softmax_state(m1, l1, a1)

    def step(c, bias):
        start = pl.multiple_of(c * tq, tq)
        k = k_ref[pl.ds(start, tq), :]
        v = v_ref[pl.ds(start, tq), :]
        _online_softmax_step(_qk(q0, k) * scale + bias, v, m0, l0, a0)
        _online_softmax_step(_qk(q1, k) * scale + bias, v, m1, l1, a1)

    step(qi, bias_ref[0, 0])

    @pl.when(qi > 0)
    def _():
        step(qi - 1, bias_ref[0, 1])

    far = far_ref[H_MOBA + h]

    def past(c, carry):
        step(c, far)
        return carry

    lax.fori_loop(0, jnp.maximum(qi - 1, 0), past, 0)
    lam = _diff_lambda(lam_ref, lam_init)
    o = _diff_finish(a0[...] / l0[...], a1[...] / l1[...], lam, g_ref[...], lam_init)
    o_ref[...] = o.astype(o_ref.dtype)


def _diff_prompt(q_maps, kb, vb, bias_tiles, far, lam_rows, g, lam_init, n_seq, name):
    m = q_maps.shape[0]
    t_len = m // n_seq
    tq = min(ATT_TILE, t_len)
    nq = t_len // tq
    h0 = H_FOX + H_MOBA
    stat = lambda: [pltpu.VMEM((tq, 1), F32), pltpu.VMEM((tq, 1), F32), pltpu.VMEM((tq, HEAD_DIM), F32)]
    grid_spec = pltpu.PrefetchScalarGridSpec(
        num_scalar_prefetch=1,
        grid=(n_seq, H_DIFF, nq),
        in_specs=[pl.BlockSpec((tq, 2 * HEAD_DIM), lambda b, h, i, far: (b * nq + i, h0 // 2 + h)),
                  pl.BlockSpec((t_len, HEAD_DIM), lambda b, h, i, far: (b, h0 + h)),
                  pl.BlockSpec((t_len, HEAD_DIM), lambda b, h, i, far: (b, h0 + h)),
                  pl.BlockSpec((1, 2, tq, tq), lambda b, h, i, far: (H_MOBA + h, 0, 0, 0)),
                  pl.BlockSpec((8, LANES), lambda b, h, i, far: (0, 0)),
                  pl.BlockSpec((1, HEAD_DIM), lambda b, h, i, far: (0, 0))],
        out_specs=pl.BlockSpec((tq, HEAD_DIM), lambda b, h, i, far: (b * nq + i, h)),
        scratch_shapes=stat() + stat(),
    )
    return pl.pallas_call(
        functools.partial(_diff_prompt_body, tq=tq, lam_init=lam_init),
        grid_spec=grid_spec,
        out_shape=jax.ShapeDtypeStruct((m, H_DIFF * HEAD_DIM), BF16),
        compiler_params=_cparams(3),
        name=name,
    )(far, q_maps, kb, vb, bias_tiles, lam_rows, g)


def _sample_page_body(pt_ref, q_ref, k_ref, v_ref, sfx_ref, cb_ref, m_ref, l_ref, acc_ref, ksum_ref):
    kp = k_ref[0, 0]
    kf = kp.reshape(PAGE_SIZE * N_HEADS, HEAD_DIM).astype(BF16)
    vf = v_ref[0, 0].reshape(PAGE_SIZE * N_HEADS, HEAD_DIM).astype(BF16)
    n_cols = PAGE_SIZE * N_HEADS
    row = lax.broadcasted_iota(jnp.int32, (SAMPLE_ROWS, n_cols), 0)
    col = lax.broadcasted_iota(jnp.int32, (SAMPLE_ROWS, n_cols), 1)
    scale = jnp.where(row < H_FOX + H_MOBA, HEAD_DIM ** -0.5, DIFF_QK_DIM ** -0.5).astype(F32)
    s = _qk(q_ref[0], kf) * scale + (sfx_ref[0, 0] + cb_ref[0, 0])
    s = jnp.where((col & (N_HEADS - 1)) == (row & (N_HEADS - 1)), s, NEG_INF)
    m = jnp.max(s, axis=1, keepdims=True)
    p = jnp.exp(s - m)
    l = jnp.sum(p, axis=1, keepdims=True)
    acc_ref[0, 0] = jnp.dot(p.astype(BF16), vf, preferred_element_type=F32)
    m_ref[0, 0] = jnp.broadcast_to(m, (SAMPLE_ROWS, LANES))
    l_ref[0, 0] = jnp.broadcast_to(l, (SAMPLE_ROWS, LANES))
    ksum_ref[0, 0] = jnp.sum(kp, axis=0)


def _sample_pages(layer, page_table, q_rows, cache_k, cache_v, sfx_rows, const_rows, name):
    n_seq, n_pages = page_table.shape
    n_cols = PAGE_SIZE * N_HEADS
    page_block = (1, 1, PAGE_SIZE, N_HEADS, HEAD_DIM)
    stat_spec = pl.BlockSpec((1, 1, SAMPLE_ROWS, LANES), lambda b, p, pt: (b, p, 0, 0))
    grid_spec = pltpu.PrefetchScalarGridSpec(
        num_scalar_prefetch=1,
        grid=(n_seq, n_pages),
        in_specs=[pl.BlockSpec((1, SAMPLE_ROWS, HEAD_DIM), lambda b, p, pt: (b, 0, 0)),
                  pl.BlockSpec(page_block, lambda b, p, pt: (layer, pt[b, p], 0, 0, 0)),
                  pl.BlockSpec(page_block, lambda b, p, pt: (layer, pt[b, p], 0, 0, 0)),
                  pl.BlockSpec((1, 1, 1, n_cols), lambda b, p, pt: (0, pt[b, p], 0, 0)),
                  pl.BlockSpec((1, 1, 1, n_cols), lambda b, p, pt: (b, p, 0, 0))],
        out_specs=[stat_spec, stat_spec, stat_spec,
                   pl.BlockSpec((1, 1, N_HEADS, HEAD_DIM), lambda b, p, pt: (b, p, 0, 0))],
    )
    stat_shape = jax.ShapeDtypeStruct((n_seq, n_pages, SAMPLE_ROWS, LANES), F32)
    return pl.pallas_call(
        _sample_page_body,
        grid_spec=grid_spec,
        out_shape=[stat_shape, stat_shape, stat_shape,
                   jax.ShapeDtypeStruct((n_seq, n_pages, N_HEADS, HEAD_DIM), F32)],
        compiler_params=_cparams(2),
        name=name,
    )(page_table, q_rows, cache_k, cache_v, sfx_rows, const_rows)


def _sample_combine_body(m_ref, l_ref, acc_ref, ksum_ref, q_ref, k_ref, v_ref, sb_ref, lam_ref, g_ref, o_ref,
                         *, n_pages, lam_init):
    n_blocks = n_pages * PAGE_SIZE // MOBA_BLOCK
    pages_per_block = MOBA_BLOCK // PAGE_SIZE
    n_sel = min(MOBA_TOPK, n_blocks)
    m = m_ref[0]
    l = l_ref[0]
    acc = acc_ref[0]
    q = q_ref[0]
    head = lax.broadcasted_iota(jnp.int32, (N_HEADS, LANES), 0)
    is_moba = jnp.where(head >= H_FOX, jnp.where(head < H_FOX + H_MOBA, 1.0, 0.0), 0.0)

    gp = jnp.sum(ksum_ref[0] * q[0:N_HEADS][None], axis=2, keepdims=True)
    gb = jnp.sum(gp.reshape(n_blocks, pages_per_block, N_HEADS, 1), axis=1) * (1.0 / MOBA_BLOCK)
    gb = jnp.broadcast_to(gb, (n_blocks, N_HEADS, LANES))
    blk = lax.broadcasted_iota(jnp.int32, (n_blocks, N_HEADS, LANES), 0)
    rank = jnp.zeros((n_blocks, N_HEADS, LANES), F32)
    for c2 in range(n_blocks):
        g2 = gb[c2:c2 + 1]
        rank = rank + jnp.where(g2 > gb, 1.0, jnp.where(g2 == gb, jnp.where(blk > c2, 1.0, 0.0), 0.0))
    sel_b = jnp.where(rank < n_sel, 1.0, 0.0)
    sel_b = jnp.maximum(sel_b, 1.0 - is_moba[None])
    sel_p = jnp.broadcast_to(sel_b[:, None], (n_blocks, pages_per_block, N_HEADS, LANES))
    sel_p = sel_p.reshape(n_pages, N_HEADS, LANES)
    sel = jnp.concatenate([sel_p, jnp.ones_like(sel_p)], axis=1)

    row = lax.broadcasted_iota(jnp.int32, (SAMPLE_ROWS, 1), 0)
    scale = jnp.where(row < H_FOX + H_MOBA, HEAD_DIM ** -0.5, DIFF_QK_DIM ** -0.5).astype(F32)
    s_self = jnp.sum(q * k_ref[0], axis=1, keepdims=True) * scale + sb_ref[0]

    m_sel = jnp.where(sel > 0.0, m, NEG_INF)
    m_all = jnp.maximum(jnp.max(m_sel, axis=0), s_self)
    w = jnp.where(sel > 0.0, jnp.exp(m_sel - m_all[None]), 0.0)
    w_self = jnp.exp(s_self - m_all)
    den = jnp.sum(w * l, axis=0) + w_self
    num = jnp.sum(w * acc, axis=0) + w_self * v_ref[0]
    o = num / den
    o_first, o_second = o[0:N_HEADS], o[N_HEADS:SAMPLE_ROWS]
    lam = _diff_lambda(lam_ref, lam_init)
    od = _diff_finish(o_first, o_second, lam, g_ref[...], lam_init)
    o_ref[0] = jnp.where(head >= H_FOX + H_MOBA, od, o_first).astype(o_ref.dtype)


def _sample_combine(m, l, acc, ksum, q_rows32, k_rows, v_rows, self_bias, lam_rows, g, lam_init, name):
    n_seq, n_pages = m.shape[:2]
    stat_spec = pl.BlockSpec((1, n_pages, SAMPLE_ROWS, LANES), lambda b: (b, 0, 0, 0))
    row_spec = pl.BlockSpec((1, SAMPLE_ROWS, HEAD_DIM), lambda b: (b, 0, 0))
    return pl.pallas_call(
        functools.partial(_sample_combine_body, n_pages=n_pages, lam_init=lam_init),
        grid=(n_seq,),
        in_specs=[stat_spec, stat_spec, stat_spec,
                  pl.BlockSpec((1, n_pages, N_HEADS, HEAD_DIM), lambda b: (b, 0, 0, 0)),
                  row_spec, row_spec, row_spec,
                  pl.BlockSpec((1, SAMPLE_ROWS, LANES), lambda b: (0, 0, 0)),
                  pl.BlockSpec((8, LANES), lambda b: (0, 0)),
                  pl.BlockSpec((1, HEAD_DIM), lambda b: (0, 0))],
        out_specs=pl.BlockSpec((1, N_HEADS, HEAD_DIM), lambda b: (b, 0, 0)),
        out_shape=jax.ShapeDtypeStruct((n_seq, N_HEADS, HEAD_DIM), BF16),
        compiler_params=_cparams(1),
        name=name,
    )(m, l, acc, ksum, q_rows32, k_rows, v_rows, self_bias, lam_rows, g)


def _router_body(x_ref, w_ref, b_ref, o_ref):
    logits = jnp.dot(x_ref[...], w_ref[...], precision=HIGHEST, preferred_element_type=F32) + b_ref[...]
    lane = lax.broadcasted_iota(jnp.int32, logits.shape, 1)
    lane_f = lane.astype(F32)
    logits = jnp.where(lane < N_EXPERTS, logits, NEG_INF)
    v1 = jnp.max(logits, axis=1, keepdims=True)
    i1 = jnp.min(jnp.where(logits == v1, lane_f, float(LANES)), axis=1, keepdims=True)
    rest = jnp.where(lane_f == i1, NEG_INF, logits)
    v2 = jnp.max(rest, axis=1, keepdims=True)
    i2 = jnp.min(jnp.where(rest == v2, lane_f, float(LANES)), axis=1, keepdims=True)
    e2 = jnp.exp(v2 - v1)
    g1 = 1.0 / (1.0 + e2)
    g2 = e2 / (1.0 + e2)
    out = jnp.where(lane == 0, i1, jnp.where(lane == 1, i2, jnp.where(lane == 2, g1, jnp.where(lane == 3, g2, 0.0))))
    o_ref[...] = out


def _router(x32, w_r, b_r, tm, name):
    m = x32.shape[0]
    tm = min(tm, m)
    return pl.pallas_call(
        _router_body,
        grid=(m // tm,),
        in_specs=[pl.BlockSpec((tm, D_MODEL), lambda i: (i, 0)),
                  pl.BlockSpec((D_MODEL, LANES), lambda i: (0, 0)),
                  pl.BlockSpec((1, LANES), lambda i: (0, 0))],
        out_specs=pl.BlockSpec((tm, LANES), lambda i: (i, 0)),
        out_shape=jax.ShapeDtypeStruct((m, LANES), F32),
        compiler_params=_cparams(1),
        name=name,
    )(x32, w_r, b_r)


def _row_copy(src_hbm, idx, dst, r, sem):
    return pltpu.make_async_copy(src_hbm.at[pl.ds(idx, 1), :], dst.at[pl.ds(r, 1), :], sem)


def _gather_rows(src_hbm, idx_ref, dst, sem, n_rows):
    def start(r, carry):
        _row_copy(src_hbm, idx_ref[0, 0, r], dst, r, sem).start()
        return carry

    def wait(r, carry):
        _row_copy(src_hbm, 0, dst, r, sem).wait()
        return carry

    lax.fori_loop(0, n_rows, start, 0)
    lax.fori_loop(0, n_rows, wait, 0)


def _dispatch_body(idx_ref, x_hbm, o_ref, buf, sem, *, tr):
    _gather_rows(x_hbm, idx_ref, buf, sem, tr)
    o_ref[...] = buf[...].astype(o_ref.dtype)


def _dispatch(x32, row_token, tr, name):
    p = row_token.shape[0]
    idx = row_token.reshape(p // tr, 1, tr)
    return pl.pallas_call(
        functools.partial(_dispatch_body, tr=tr),
        grid=(p // tr,),
        in_specs=[pl.BlockSpec((1, 1, tr), lambda i: (i, 0, 0), memory_space=pltpu.SMEM),
                  pl.BlockSpec(memory_space=pl.ANY)],
        out_specs=pl.BlockSpec((tr, D_MODEL), lambda i: (i, 0)),
        out_shape=jax.ShapeDtypeStruct((p, D_MODEL), BF16),
        scratch_shapes=[pltpu.VMEM((tr, D_MODEL), F32), pltpu.SemaphoreType.DMA(())],
        compiler_params=_cparams(1),
        name=name,
    )(idx, x32)


def _expert_up_body(te_ref, nu_ref, x_ref, wg_ref, wu_ref, h_ref):
    @pl.when(pl.program_id(0) < nu_ref[0])
    def _():
        x = x_ref[...]
        g = jnp.dot(x, wg_ref[0], preferred_element_type=F32)
        u = jnp.dot(x, wu_ref[0], preferred_element_type=F32)
        h_ref[...] = (g * (1.0 / (1.0 + jnp.exp(-g))) * u).astype(h_ref.dtype)

    @pl.when(pl.program_id(0) >= nu_ref[0])
    def _():
        h_ref[...] = jnp.zeros_like(h_ref)


def _expert_up(xs, wg, wu, tile_expert, n_used, tm, tn, name):
    p, k = xs.shape
    f = wg.shape[2]
    tn = min(tn, f)
    nf = f // tn
    wspec = pl.BlockSpec((1, k, tn), lambda i, j, te, nu: (te[i], 0, jnp.where(i < nu[0], j, nf - 1)))
    grid_spec = pltpu.PrefetchScalarGridSpec(
        num_scalar_prefetch=2,
        grid=(p // tm, nf),
        in_specs=[pl.BlockSpec((tm, k), lambda i, j, te, nu: (i, 0)), wspec, wspec],
        out_specs=pl.BlockSpec((tm, tn), lambda i, j, te, nu: (i, j)),
    )
    return pl.pallas_call(
        _expert_up_body,
        grid_spec=grid_spec,
        out_shape=jax.ShapeDtypeStruct((p, f), BF16),
        compiler_params=_cparams(2),
        name=name,
    )(tile_expert, n_used, xs, wg, wu)


def _expert_down_body(te_ref, nu_ref, h_ref, w_ref, y_ref, acc):
    kk = pl.program_id(1)

    @pl.when(pl.program_id(0) < nu_ref[0])
    def _():
        @pl.when(kk == 0)
        def _():
            acc[...] = jnp.zeros_like(acc)

        acc[...] += jnp.dot(h_ref[...], w_ref[0], preferred_element_type=F32)

        @pl.when(kk == pl.num_programs(1) - 1)
        def _():
            y_ref[...] = acc[...]

    @pl.when(pl.program_id(0) >= nu_ref[0])
    def _():
        y_ref[...] = jnp.zeros_like(y_ref)


def _expert_down(h, wd, tile_expert, n_used, tm, tk, name):
    p, f = h.shape
    tk = min(tk, f)
    nk = f // tk
    grid_spec = pltpu.PrefetchScalarGridSpec(
        num_scalar_prefetch=2,
        grid=(p // tm, nk),
        in_specs=[pl.BlockSpec((tm, tk), lambda i, k, te, nu: (i, k)),
                  pl.BlockSpec((1, tk, D_MODEL), lambda i, k, te, nu: (te[i], jnp.where(i < nu[0], k, nk - 1), 0))],
        out_specs=pl.BlockSpec((tm, D_MODEL), lambda i, k, te, nu: (i, 0)),
        scratch_shapes=[pltpu.VMEM((tm, D_MODEL), F32)],
    )
    return pl.pallas_call(
        _expert_down_body,
        grid_spec=grid_spec,
        out_shape=jax.ShapeDtypeStruct((p, D_MODEL), F32),
        compiler_params=_cparams(2),
        name=name,
    )(tile_expert, n_used, h, wd)


def _moe_combine_body(p0_ref, p1_ref, y_hbm, r_ref, x_ref, g_ref, b_ref, o32_ref, ob_ref, buf0, buf1, sem0, sem1,
                      *, tr):
    _gather_rows(y_hbm, p0_ref, buf0, sem0, tr)
    _gather_rows(y_hbm, p1_ref, buf1, sem1, tr)
    r = r_ref[...]
    y = ALPHA * x_ref[...] + (r[:, 2:3] * buf0[...] + r[:, 3:4] * buf1[...])
    y = _layer_norm_rows(y, g_ref[...], b_ref[...])
    o32_ref[...] = y
    ob_ref[...] = y.astype(BF16)


def _moe_combine(y_sorted, pos0, pos1, routes, x32, g, b, tr, name):
    m = x32.shape[0]
    tr = min(tr, m)
    idx_spec = pl.BlockSpec((1, 1, tr), lambda i: (i, 0, 0), memory_space=pltpu.SMEM)
    row = lambda i: (i, 0)
    fixed = lambda i: (0, 0)
    return pl.pallas_call(
        functools.partial(_moe_combine_body, tr=tr),
        grid=(m // tr,),
        in_specs=[idx_spec, idx_spec,
                  pl.BlockSpec(memory_space=pl.ANY),
                  pl.BlockSpec((tr, LANES), row),
                  pl.BlockSpec((tr, D_MODEL), row),
                  pl.BlockSpec((1, D_MODEL), fixed),
                  pl.BlockSpec((1, D_MODEL), fixed)],
        out_specs=[pl.BlockSpec((tr, D_MODEL), row)] * 2,
        out_shape=[jax.ShapeDtypeStruct((m, D_MODEL), F32), jax.ShapeDtypeStruct((m, D_MODEL), BF16)],
        scratch_shapes=[pltpu.VMEM((tr, D_MODEL), F32), pltpu.VMEM((tr, D_MODEL), F32),
                        pltpu.SemaphoreType.DMA(()), pltpu.SemaphoreType.DMA(())],
        compiler_params=_cparams(1),
        name=name,
    )(pos0.reshape(m // tr, 1, tr), pos1.reshape(m // tr, 1, tr), y_sorted, routes, x32, g, b)


def _routing_plan(routes, tm):
    m = routes.shape[0]
    experts = jnp.concatenate([routes[:, 0], routes[:, 1]]).astype(jnp.int32)
    onehot = (experts[:, None] == jnp.arange(N_EXPERTS, dtype=jnp.int32)[None, :]).astype(jnp.int32)
    csum = jnp.cumsum(onehot, axis=0)
    rank = jnp.sum(csum * onehot, axis=1) - 1
    counts = csum[-1]
    padded = ((counts + tm - 1) // tm) * tm
    ends = jnp.cumsum(padded)
    starts = ends - padded
    pos = jnp.sum(onehot * starts[None, :], axis=1) + rank
    n_rows = (pl.cdiv(TOP_K * m, tm) + N_EXPERTS) * tm
    token = jnp.concatenate([jnp.arange(m, dtype=jnp.int32)] * TOP_K)
    row_token = jnp.zeros((n_rows,), jnp.int32).at[pos].set(token)
    tile_start = jnp.arange(n_rows // tm, dtype=jnp.int32) * tm
    tile_expert = jnp.minimum(jnp.sum((tile_start[:, None] >= ends[None, :]).astype(jnp.int32), axis=1),
                              N_EXPERTS - 1)
    n_used = (ends[-1] // tm).reshape(1).astype(jnp.int32)
    return row_token, pos[:m], pos[m:], tile_expert, n_used


def _moe_layer(x32, w_r, b_r, wg, wu, wd, g, b, tm, tn, tk, tag):
    routes = _router(x32, w_r, b_r, 512, f"router_{tag}")
    row_token, pos0, pos1, tile_expert, n_used = _routing_plan(routes, tm)
    xs = _dispatch(x32, row_token, tm, f"moe_dispatch_{tag}")
    h = _expert_up(xs, wg, wu, tile_expert, n_used, tm, tn, f"moe_up_{tag}")
    y = _expert_down(h, wd, tile_expert, n_used, tm, tk, f"moe_down_{tag}")
    return _moe_combine(y, pos0, pos1, routes, x32, g, b, 256, f"moe_combine_{tag}")


def _q_map_weights(w_q):
    h0 = (H_FOX + H_MOBA) * HEAD_DIM
    cols = [w_q[:, :h0]]
    zeros = jnp.zeros((w_q.shape[0], DIFF_QK_DIM), w_q.dtype)
    for i in range(H_DIFF):
        wh = w_q[:, h0 + i * HEAD_DIM:h0 + (i + 1) * HEAD_DIM]
        cols += [wh[:, :DIFF_QK_DIM], zeros, zeros, wh[:, DIFF_QK_DIM:]]
    return jnp.concatenate(cols, axis=1)


def _sample_row_order():
    first = list(range(H_FOX + H_MOBA)) + [H_FOX + H_MOBA + 2 * i for i in range(H_DIFF)]
    second = [H_FOX + H_MOBA + 2 * i + 1 for i in range(H_DIFF)]
    return first, second


def kernel(x_prompt, x_sample, cache_k, cache_v, cache_logf, page_table, w_in, b_forget, w_out, lam_q1, lam_k1, lam_q2, lam_k2, subln_g, rel_bias, ln1_g, ln1_b, ln2_g, ln2_b, w_ffn_gate, w_ffn_up, w_ffn_down, w_router, b_router, w_exp_gate, w_exp_up, w_exp_down):
    n_seq, t_len, _ = x_prompt.shape
    n_dec, dec_seq, _ = x_sample.shape
    assert dec_seq == 1 and t_len % MOBA_BLOCK == 0
    n_pages = page_table.shape[1]
    past = n_pages * PAGE_SIZE
    assert past % MOBA_BLOCK == 0 and past >= MOBA_BLOCK
    depth = w_in.shape[0]
    n_pool = cache_k.shape[1]
    mp, ms = n_seq * t_len, n_dec
    n_cols = PAGE_SIZE * N_HEADS

    rel_t = jnp.transpose(rel_bias).astype(F32)
    far = rel_t[:, N_BUCKETS - 1]
    bias_tiles = _bias_tiles(rel_t, ATT_TILE)
    dist = past - (jnp.arange(n_pages)[:, None] * PAGE_SIZE + jnp.arange(PAGE_SIZE)[None, :])
    bucket_of = jnp.array([_t5_bucket_static(d) for d in range(T5_FAR + 1)], jnp.int32)
    bkt = bucket_of[jnp.minimum(dist, T5_FAR)]
    t5_cols = jnp.concatenate([jnp.zeros((n_pages, PAGE_SIZE, H_FOX), F32), rel_bias[bkt]], axis=2)
    first_rows, second_rows = _sample_row_order()
    self_bias16 = jnp.concatenate([jnp.zeros((H_FOX,), F32), rel_bias[0]])
    self_bias = jnp.broadcast_to(jnp.concatenate([self_bias16, self_bias16])[None, :, None],
                                 (1, SAMPLE_ROWS, LANES))

    xp32 = x_prompt.reshape(mp, D_MODEL)
    xs32 = x_sample.reshape(ms, D_MODEL)
    xpb, xsb = xp32.astype(BF16), xs32.astype(BF16)
    outs = {k: [] for k in ("kp", "vp", "fp", "ks", "vs", "fs")}

    for l in range(depth):
        lam_init = 0.8 - 0.6 * math.exp(-0.3 * l)
        w_q = _q_map_weights(w_in[l, :, :D_MODEL]).astype(BF16)
        w_k = w_in[l, :, D_MODEL:2 * D_MODEL].astype(BF16)
        w_v = w_in[l, :, 2 * D_MODEL:3 * D_MODEL].astype(BF16)
        w_f = jnp.pad(w_in[l, :, 3 * D_MODEL:], ((0, 0), (0, LANES - H_FOX)))
        b_f = jnp.pad(b_forget[l], (0, LANES - H_FOX)).reshape(1, LANES)
        w_o = w_out[l].astype(BF16)
        lam_rows = jnp.pad(jnp.stack([lam_q1[l], lam_k1[l], lam_q2[l], lam_k2[l]]),
                           ((0, 4), (0, LANES - DIFF_QK_DIM)))
        g_sub = subln_g[l].reshape(1, HEAD_DIM)
        g1, b1 = ln1_g[l].reshape(1, D_MODEL), ln1_b[l].reshape(1, D_MODEL)
        g2, b2 = ln2_g[l].reshape(1, D_MODEL), ln2_b[l].reshape(1, D_MODEL)

        (qp,) = _matmul(xpb, w_q, [BF16], 512, 512, f"proj_q_p{l}")
        kp32, kpb = _matmul(xpb, w_k, [F32, BF16], 512, 512, f"proj_k_p{l}")
        vp32, vpb = _matmul(xpb, w_v, [F32, BF16], 512, 512, f"proj_v_p{l}")
        lfp, fcum = _log_forget(xp32, w_f, b_f, n_seq, True, f"logf_p{l}")
        f_cols = jnp.transpose(fcum.reshape(n_seq, t_len, LANES)[:, :, :8], (0, 2, 1)).reshape(n_seq, 8, 1, t_len)
        o_fox = _fox_prompt(qp, kpb, vpb, fcum, f_cols, n_seq, f"fox_p{l}")
        means = _block_means(kp32, n_seq, f"moba_means_p{l}")
        o_moba = _moba_prompt(qp, kpb, vpb, means, bias_tiles, far, n_seq, f"moba_p{l}")
        o_diff = _diff_prompt(qp, kpb, vpb, bias_tiles, far, lam_rows, g_sub, lam_init, n_seq, f"diff_p{l}")
        xp32, xpb = _outproj_ln(o_fox, o_moba, o_diff, w_o, xp32, g1, b1, 256, f"outproj_p{l}")

        (qs,) = _matmul(xsb, w_q, [F32], 512, 512, f"proj_q_s{l}")
        ks32, _ = _matmul(xsb, w_k, [F32, BF16], 512, 512, f"proj_k_s{l}")
        vs32, _ = _matmul(xsb, w_v, [F32, BF16], 512, 512, f"proj_v_s{l}")
        lfs, _ = _log_forget(xs32, w_f, b_f, n_dec, False, f"logf_s{l}")
        qs = qs.reshape(ms, N_QMAPS, HEAD_DIM)
        pad_rows = jnp.zeros((ms, SAMPLE_ROWS - N_HEADS - H_DIFF, HEAD_DIM), F32)
        q_rows32 = jnp.concatenate([qs[:, jnp.array(first_rows)], pad_rows, qs[:, jnp.array(second_rows)]], axis=1)
        k_heads = ks32.reshape(ms, N_HEADS, HEAD_DIM)
        v_heads = vs32.reshape(ms, N_HEADS, HEAD_DIM)
        k_rows = jnp.concatenate([k_heads, k_heads], axis=1)
        v_rows = jnp.concatenate([v_heads, v_heads], axis=1)
        lf_pool = cache_logf[l].astype(F32)
        sfx_pool = jnp.flip(jnp.cumsum(jnp.flip(lf_pool, 1), 1), 1) - lf_pool
        sfx_rows = jnp.pad(sfx_pool, ((0, 0), (0, 0), (0, N_HEADS - H_FOX))).reshape(1, n_pool, 1, n_cols)
        page_tot = jnp.sum(lf_pool, axis=1)[page_table]
        later = jnp.flip(jnp.cumsum(jnp.flip(page_tot, 1), 1), 1) - page_tot
        fox_const = later + lfs[:, None, :H_FOX]
        const16 = jnp.concatenate(
            [jnp.broadcast_to(fox_const[:, :, None, :], (ms, n_pages, PAGE_SIZE, H_FOX)),
             jnp.broadcast_to(t5_cols[None, :, :, H_FOX:], (ms, n_pages, PAGE_SIZE, N_HEADS - H_FOX))], axis=3)
        const_rows = const16.reshape(ms, n_pages, 1, n_cols)
        m_s, l_s, acc_s, ksum = _sample_pages(l, page_table, q_rows32.astype(BF16), cache_k, cache_v,
                                              sfx_rows, const_rows, f"sample_pages{l}")
        o_s = _sample_combine(m_s, l_s, acc_s, ksum, q_rows32, k_rows, v_rows, self_bias, lam_rows, g_sub,
                              lam_init, f"sample_combine{l}").reshape(ms, D_MODEL)
        nf, nm = H_FOX * HEAD_DIM, (H_FOX + H_MOBA) * HEAD_DIM
        xs32, xsb = _outproj_ln(o_s[:, :nf], o_s[:, nf:nm], o_s[:, nm:], w_o, xs32, g1, b1, 256, f"outproj_s{l}")

        i = l // 2
        if l % 2 == 0:
            wg, wu, wd = (w_ffn_gate[i].astype(BF16), w_ffn_up[i].astype(BF16), w_ffn_down[i].astype(BF16))
            hp = _swiglu_up(xpb, wg, wu, 1024, 512, f"ffn_up_p{l}")
            xp32, xpb = _down_ln(hp, wd, xp32, g2, b2, 512, 1408, f"ffn_down_p{l}")
            hs = _swiglu_up(xsb, wg, wu, 1024, 512, f"ffn_up_s{l}")
            xs32, xsb = _down_ln(hs, wd, xs32, g2, b2, 512, 1408, f"ffn_down_s{l}")
        else:
            wg, wu, wd = (w_exp_gate[i].astype(BF16), w_exp_up[i].astype(BF16), w_exp_down[i].astype(BF16))
            w_r = jnp.pad(w_router[i], ((0, 0), (0, LANES - N_EXPERTS)))
            b_r = jnp.pad(b_router[i], (0, LANES - N_EXPERTS)).reshape(1, LANES)
            xp32, xpb = _moe_layer(xp32, w_r, b_r, wg, wu, wd, g2, b2, 512, 512, 1408, f"p{l}")
            xs32, xsb = _moe_layer(xs32, w_r, b_r, wg, wu, wd, g2, b2, 16, 512, 1408, f"s{l}")

        outs["kp"].append(kp32.reshape(n_seq, t_len // PAGE_SIZE, PAGE_SIZE, N_HEADS, HEAD_DIM))
        outs["vp"].append(vp32.reshape(n_seq, t_len // PAGE_SIZE, PAGE_SIZE, N_HEADS, HEAD_DIM))
        outs["fp"].append(lfp[:, :H_FOX].reshape(n_seq, t_len // PAGE_SIZE, PAGE_SIZE, H_FOX))
        outs["ks"].append(ks32.reshape(n_dec, 1, N_HEADS, HEAD_DIM))
        outs["vs"].append(vs32.reshape(n_dec, 1, N_HEADS, HEAD_DIM))
        outs["fs"].append(lfs[:, :H_FOX].reshape(n_dec, 1, H_FOX))

    return (xp32.reshape(n_seq, t_len, D_MODEL), xs32.reshape(n_dec, 1, D_MODEL),
            jnp.stack(outs["kp"]), jnp.stack(outs["vp"]), jnp.stack(outs["fp"]),
            jnp.stack(outs["ks"]), jnp.stack(outs["vs"]), jnp.stack(outs["fs"]))
```

```python
import functools
import math

import jax
import jax.numpy as jnp
from jax import lax
from jax.experimental import pallas as pl
from jax.experimental.pallas import tpu as pltpu

F32 = jnp.float32
BF16 = jnp.bfloat16
HIGHEST = lax.Precision.HIGHEST
NEG_INF = float("-inf")

HEAD_DIM = 128
N_HEADS = 16
D_MODEL = N_HEADS * HEAD_DIM
H_FOX = 6
H_MOBA = 6
H_DIFF = 4
DIFF_QK_DIM = HEAD_DIM // 2
N_QMAPS = H_FOX + H_MOBA + 2 * H_DIFF
MOBA_BLOCK = 256
MOBA_TOPK = 3
PAGE_SIZE = 128
N_BUCKETS = 32
MAX_DISTANCE = 128
N_EXPERTS = 8
TOP_K = 2
DEPTH = 2
ALPHA = (2.0 * DEPTH) ** 0.25
LN_EPS = 1e-5

LANES = 128
VMEM_LIMIT_BYTES = 56 * 1024 * 1024
ATT_TILE = 2 * MOBA_BLOCK
LOGF_TILE = 256
SAMPLE_ROWS = 2 * N_HEADS
SAMPLE_PAGES_PER_STEP = 4
GATHER_UNROLL = 8


def _cparams(n_axes):
    return pltpu.CompilerParams(
        dimension_semantics=("arbitrary",) * n_axes, vmem_limit_bytes=VMEM_LIMIT_BYTES)


def _t5_thresholds():
    max_exact = N_BUCKETS // 2

    def bucket(d):
        if d < max_exact:
            return d
        r = math.log(d / max_exact) / math.log(MAX_DISTANCE / max_exact)
        return min(max_exact + int(r * (N_BUCKETS - max_exact)), N_BUCKETS - 1)

    return [min(d for d in range(2 * MAX_DISTANCE) if bucket(d) >= b) for b in range(N_BUCKETS)]


T5_THRESHOLDS = _t5_thresholds()
T5_FAR = T5_THRESHOLDS[-1]


def _t5_bucket_static(d):
    b = 0
    for i, t in enumerate(T5_THRESHOLDS):
        if d >= t:
            b = i
    return b


def _mm_body(x_ref, w_ref, *o_refs):
    acc = jnp.dot(x_ref[...], w_ref[...], preferred_element_type=F32)
    for o in o_refs:
        o[...] = acc.astype(o.dtype)


def _matmul(x, w, out_dtypes, tm, tn, name):
    m, k = x.shape
    n = w.shape[1]
    tm, tn = min(tm, m), min(tn, n)
    return pl.pallas_call(
        _mm_body,
        grid=(m // tm, n // tn),
        in_specs=[pl.BlockSpec((tm, k), lambda i, j: (i, 0)),
                  pl.BlockSpec((k, tn), lambda i, j: (0, j))],
        out_specs=[pl.BlockSpec((tm, tn), lambda i, j: (i, j)) for _ in out_dtypes],
        out_shape=[jax.ShapeDtypeStruct((m, n), d) for d in out_dtypes],
        compiler_params=_cparams(2),
        name=name,
    )(x, w)


def _log_sigmoid(z):
    return jnp.minimum(z, 0.0) - jnp.log(1.0 + jnp.exp(-jnp.abs(z)))


def _logf_body(x_ref, w_ref, b_ref, lf_ref, cum_ref, carry, *, tt, cumulative):
    t = pl.program_id(1)
    z = jnp.dot(x_ref[...], w_ref[...], precision=HIGHEST, preferred_element_type=F32) + b_ref[...]
    lf = _log_sigmoid(z)
    lf_ref[...] = lf
    if cumulative:
        @pl.when(t == 0)
        def _():
            carry[...] = jnp.zeros_like(carry)

        row = lax.broadcasted_iota(jnp.int32, (tt, tt), 0)
        col = lax.broadcasted_iota(jnp.int32, (tt, tt), 1)
        tri = jnp.where(col <= row, 1.0, 0.0).astype(F32)
        cum = jnp.dot(tri, lf, precision=HIGHEST, preferred_element_type=F32) + carry[...]
        cum_ref[...] = cum
        carry[...] = cum[tt - 1:tt, :]
    else:
        cum_ref[...] = lf


def _log_forget(x32, w_f, b_f, n_seq, cumulative, name):
    m = x32.shape[0]
    t_len = m // n_seq
    tt = min(LOGF_TILE, t_len) if cumulative else m
    n_seq_grid = n_seq if cumulative else 1
    nt = (m // n_seq_grid) // tt
    spec = pl.BlockSpec((tt, LANES), lambda b, t: (b * nt + t, 0))
    return pl.pallas_call(
        functools.partial(_logf_body, tt=tt, cumulative=cumulative),
        grid=(n_seq_grid, nt),
        in_specs=[pl.BlockSpec((tt, D_MODEL), lambda b, t: (b * nt + t, 0)),
                  pl.BlockSpec((D_MODEL, LANES), lambda b, t: (0, 0)),
                  pl.BlockSpec((1, LANES), lambda b, t: (0, 0))],
        out_specs=[spec, spec],
        out_shape=[jax.ShapeDtypeStruct((m, LANES), F32)] * 2,
        scratch_shapes=[pltpu.VMEM((1, LANES), F32)],
        compiler_params=_cparams(2),
        name=name,
    )(x32, w_f, b_f)


def _layer_norm_rows(y, g, b):
    mean = jnp.mean(y, axis=1, keepdims=True)
    yc = y - mean
    var = jnp.mean(yc * yc, axis=1, keepdims=True)
    return yc * lax.rsqrt(var + LN_EPS) * g + b


def _outproj_ln_body(of_ref, om_ref, od_ref, w_ref, x_ref, g_ref, b_ref, o32_ref, ob_ref):
    nf, nm = H_FOX * HEAD_DIM, (H_FOX + H_MOBA) * HEAD_DIM
    a = jnp.dot(of_ref[...], w_ref[0:nf, :], preferred_element_type=F32)
    a += jnp.dot(om_ref[...], w_ref[nf:nm, :], preferred_element_type=F32)
    a += jnp.dot(od_ref[...], w_ref[nm:D_MODEL, :], preferred_element_type=F32)
    y = _layer_norm_rows(ALPHA * x_ref[...] + a, g_ref[...], b_ref[...])
    o32_ref[...] = y
    ob_ref[...] = y.astype(BF16)


def _outproj_ln(o_fox, o_moba, o_diff, w_out, x32, g, b, tm, name):
    m = x32.shape[0]
    tm = min(tm, m)
    row = lambda i: (i, 0)
    fixed = lambda i: (0, 0)
    return pl.pallas_call(
        _outproj_ln_body,
        grid=(m // tm,),
        in_specs=[pl.BlockSpec((tm, o_fox.shape[1]), row),
                  pl.BlockSpec((tm, o_moba.shape[1]), row),
                  pl.BlockSpec((tm, o_diff.shape[1]), row),
                  pl.BlockSpec((D_MODEL, D_MODEL), fixed),
                  pl.BlockSpec((tm, D_MODEL), row),
                  pl.BlockSpec((1, D_MODEL), fixed),
                  pl.BlockSpec((1, D_MODEL), fixed)],
        out_specs=[pl.BlockSpec((tm, D_MODEL), row)] * 2,
        out_shape=[jax.ShapeDtypeStruct((m, D_MODEL), F32), jax.ShapeDtypeStruct((m, D_MODEL), BF16)],
        compiler_params=_cparams(1),
        name=name,
    )(o_fox, o_moba, o_diff, w_out, x32, g, b)


def _swiglu_up_body(x_ref, wg_ref, wu_ref, h_ref):
    x = x_ref[...]
    g = jnp.dot(x, wg_ref[...], preferred_element_type=F32)
    u = jnp.dot(x, wu_ref[...], preferred_element_type=F32)
    h_ref[...] = (g * (1.0 / (1.0 + jnp.exp(-g))) * u).astype(h_ref.dtype)


def _swiglu_up(xb, wg, wu, tm, tn, name):
    m, k = xb.shape
    f = wg.shape[1]
    tm, tn = min(tm, m), min(tn, f)
    return pl.pallas_call(
        _swiglu_up_body,
        grid=(m // tm, f // tn),
        in_specs=[pl.BlockSpec((tm, k), lambda i, j: (i, 0)),
                  pl.BlockSpec((k, tn), lambda i, j: (0, j)),
                  pl.BlockSpec((k, tn), lambda i, j: (0, j))],
        out_specs=pl.BlockSpec((tm, tn), lambda i, j: (i, j)),
        out_shape=jax.ShapeDtypeStruct((m, f), BF16),
        compiler_params=_cparams(2),
        name=name,
    )(xb, wg, wu)


def _down_ln_body(h_ref, w_ref, x_ref, g_ref, b_ref, o32_ref, ob_ref, acc):
    kk = pl.program_id(1)

    @pl.when(kk == 0)
    def _():
        acc[...] = jnp.zeros_like(acc)

    acc[...] += jnp.dot(h_ref[...], w_ref[...], preferred_element_type=F32)

    @pl.when(kk == pl.num_programs(1) - 1)
    def _():
        y = _layer_norm_rows(ALPHA * x_ref[...] + acc[...], g_ref[...], b_ref[...])
        o32_ref[...] = y
        ob_ref[...] = y.astype(BF16)


def _down_ln(h, wd, x32, g, b, tm, tk, name):
    m, f = h.shape
    tm, tk = min(tm, m), min(tk, f)
    row = lambda i, k: (i, 0)
    fixed = lambda i, k: (0, 0)
    return pl.pallas_call(
        _down_ln_body,
        grid=(m // tm, f // tk),
        in_specs=[pl.BlockSpec((tm, tk), lambda i, k: (i, k)),
                  pl.BlockSpec((tk, D_MODEL), lambda i, k: (k, 0)),
                  pl.BlockSpec((tm, D_MODEL), row),
                  pl.BlockSpec((1, D_MODEL), fixed),
                  pl.BlockSpec((1, D_MODEL), fixed)],
        out_specs=[pl.BlockSpec((tm, D_MODEL), row)] * 2,
        out_shape=[jax.ShapeDtypeStruct((m, D_MODEL), F32), jax.ShapeDtypeStruct((m, D_MODEL), BF16)],
        scratch_shapes=[pltpu.VMEM((tm, D_MODEL), F32)],
        compiler_params=_cparams(2),
        name=name,
    )(h, wd, x32, g, b)


def _online_softmax_step(s, v, m_sc, l_sc, acc_sc):
    m_prev = m_sc[...]
    m_new = jnp.maximum(m_prev, jnp.max(s, axis=1, keepdims=True))
    alpha = jnp.exp(m_prev - m_new)
    p = jnp.exp(s - jnp.tile(m_new, (1, s.shape[1] // LANES)))
    l_sc[...] = alpha * l_sc[...] + jnp.sum(p, axis=1, keepdims=True)
    acc_sc[...] = alpha * acc_sc[...] + jnp.dot(p.astype(BF16), v, preferred_element_type=F32)
    m_sc[...] = m_new


def _softmax_scratch(tq):
    return [pltpu.VMEM((tq, LANES), F32), pltpu.VMEM((tq, LANES), F32), pltpu.VMEM((tq, HEAD_DIM), F32)]


def _init_softmax_state(m_sc, l_sc, acc_sc):
    m_sc[...] = jnp.full(m_sc.shape, NEG_INF, F32)
    l_sc[...] = jnp.zeros_like(l_sc)
    acc_sc[...] = jnp.zeros_like(acc_sc)


def _qk(q, k):
    return lax.dot_general(q, k, (((1,), (1,)), ((), ())), preferred_element_type=F32)


def _lane_column(x, idx):
    lane = lax.broadcasted_iota(jnp.int32, x.shape, 1)
    return jnp.sum(jnp.where(lane == idx, x, 0.0), axis=1, keepdims=True)


def _fox_prompt_body(q_ref, k_ref, v_ref, frow_ref, fcol_ref, o_ref, m_sc, l_sc, acc_sc, *, tq):
    h = pl.program_id(1)
    qi = pl.program_id(2)
    scale = HEAD_DIM ** -0.5
    q = q_ref[...]
    frow = jnp.broadcast_to(_lane_column(frow_ref[...], h), (tq, LANES))
    _init_softmax_state(m_sc, l_sc, acc_sc)

    def step(c, diagonal):
        start = pl.multiple_of(c * tq, tq)
        s = _qk(q, k_ref[pl.ds(start, tq), :]) * scale
        s = s + (jnp.tile(frow, (1, tq // LANES)) - fcol_ref[0, 0, :, pl.ds(start, tq)])
        if diagonal:
            row = lax.broadcasted_iota(jnp.int32, (tq, tq), 0)
            col = lax.broadcasted_iota(jnp.int32, (tq, tq), 1)
            s = jnp.where(col <= row, s, NEG_INF)
        _online_softmax_step(s, v_ref[pl.ds(start, tq), :], m_sc, l_sc, acc_sc)

    step(qi, True)

    def past(c, carry):
        step(c, False)
        return carry

    lax.fori_loop(0, qi, past, 0)
    o_ref[...] = (acc_sc[...] / l_sc[...]).astype(o_ref.dtype)


def _fox_prompt(q_maps, kb, vb, f_rows, f_cols, n_seq, name):
    m = q_maps.shape[0]
    t_len = m // n_seq
    tq = min(ATT_TILE, t_len)
    nq = t_len // tq
    return pl.pallas_call(
        functools.partial(_fox_prompt_body, tq=tq),
        grid=(n_seq, H_FOX, nq),
        in_specs=[pl.BlockSpec((tq, HEAD_DIM), lambda b, h, i: (b * nq + i, h)),
                  pl.BlockSpec((t_len, HEAD_DIM), lambda b, h, i: (b, h)),
                  pl.BlockSpec((t_len, HEAD_DIM), lambda b, h, i: (b, h)),
                  pl.BlockSpec((tq, LANES), lambda b, h, i: (b * nq + i, 0)),
                  pl.BlockSpec((1, 1, 1, t_len), lambda b, h, i: (b, h, 0, 0))],
        out_specs=pl.BlockSpec((tq, HEAD_DIM), lambda b, h, i: (b * nq + i, h)),
        out_shape=jax.ShapeDtypeStruct((m, H_FOX * HEAD_DIM), BF16),
        scratch_shapes=_softmax_scratch(tq),
        compiler_params=_cparams(3),
        name=name,
    )(q_maps, kb, vb, f_rows, f_cols)


def _bias_tiles_body(rel_ref, o_ref, *, tq):
    h = pl.program_id(0)
    row = lax.broadcasted_iota(jnp.int32, (tq, tq), 0)
    col = lax.broadcasted_iota(jnp.int32, (tq, tq), 1)
    for kind in range(2):
        d = row - col + kind * tq
        bias = jnp.full((tq, tq), rel_ref[h, 0], F32)
        for bkt in range(1, N_BUCKETS):
            bias = jnp.where(d >= T5_THRESHOLDS[bkt], rel_ref[h, bkt], bias)
        if kind == 0:
            bias = jnp.where(d >= 0, bias, NEG_INF)
        o_ref[0, kind] = bias


def _bias_tiles(rel_t, tq):
    nh = rel_t.shape[0]
    return pl.pallas_call(
        functools.partial(_bias_tiles_body, tq=tq),
        grid=(nh,),
        in_specs=[pl.BlockSpec(memory_space=pltpu.SMEM)],
        out_specs=pl.BlockSpec((1, 2, tq, tq), lambda h: (h, 0, 0, 0)),
        out_shape=jax.ShapeDtypeStruct((nh, 2, tq, tq), F32),
        compiler_params=_cparams(1),
        name="t5_bias_tiles",
    )(rel_t)


def _block_means_body(k_ref, o_ref, *, n_blocks):
    k = k_ref[...]
    o_ref[...] = jnp.zeros_like(o_ref)
    km = jnp.mean(k.reshape(n_blocks, MOBA_BLOCK, HEAD_DIM), axis=1)
    o_ref[0, 0, 0:n_blocks, :] = km


def _block_means(k32, n_seq, name):
    m = k32.shape[0]
    t_len = m // n_seq
    n_blocks = t_len // MOBA_BLOCK
    return pl.pallas_call(
        functools.partial(_block_means_body, n_blocks=n_blocks),
        grid=(n_seq, H_MOBA),
        in_specs=[pl.BlockSpec((t_len, HEAD_DIM), lambda b, h: (b, H_FOX + h))],
        out_specs=pl.BlockSpec((1, 1, LANES, HEAD_DIM), lambda b, h: (b, h, 0, 0)),
        out_shape=jax.ShapeDtypeStruct((n_seq, H_MOBA, LANES, HEAD_DIM), F32),
        compiler_params=_cparams(2),
        name=name,
    )(k32)


def _top_blocks(gate, n_valid, n_candidates, n_sel):
    lane = lax.broadcasted_iota(jnp.int32, gate.shape, 1)
    rank = jnp.zeros(gate.shape, F32)
    for c2 in range(n_candidates):
        g2 = gate[:, c2:c2 + 1]
        beats = jnp.where(g2 > gate, 1.0, jnp.where(g2 == gate, jnp.where(lane > c2, 1.0, 0.0), 0.0))
        rank = rank + beats * jnp.where(c2 < n_valid, 1.0, 0.0)
    return jnp.where(rank < n_sel, jnp.where(lane < n_valid, 1.0, 0.0), 0.0)


def _moba_prompt_body(far_ref, q_ref, k_ref, v_ref, means_ref, bias_ref, o_ref, m_sc, l_sc, acc_sc, mask_sc,
                      *, tq, t_len, n_candidates, n_sel):
    h = pl.program_id(1)
    qi = pl.program_id(2)
    scale = HEAD_DIM ** -0.5
    q = q_ref[...]
    gate = lax.dot_general(q.astype(F32), means_ref[0, 0], (((1,), (1,)), ((), ())),
                           precision=HIGHEST, preferred_element_type=F32)
    row = lax.broadcasted_iota(jnp.int32, (tq, LANES), 0)
    lane = lax.broadcasted_iota(jnp.int32, (tq, LANES), 1)
    own = qi * (tq // MOBA_BLOCK) + lax.div(row, MOBA_BLOCK)
    sel = jnp.where(lane == own, 1.0, _top_blocks(gate, own, n_candidates, n_sel))
    blk = lax.broadcasted_iota(jnp.int32, (LANES, t_len), 0)
    col_blk = lax.div(lax.broadcasted_iota(jnp.int32, (LANES, t_len), 1), MOBA_BLOCK)
    covers = jnp.where(blk == col_blk, 1.0, 0.0).astype(BF16)
    picked = jnp.dot(sel.astype(BF16), covers, preferred_element_type=F32)
    mask_sc[...] = jnp.where(picked > 0.5, 0.0, NEG_INF)
    _init_softmax_state(m_sc, l_sc, acc_sc)

    def step(c, bias):
        start = pl.multiple_of(c * tq, tq)
        s = _qk(q, k_ref[pl.ds(start, tq), :]) * scale + bias + mask_sc[:, pl.ds(start, tq)]
        _online_softmax_step(s, v_ref[pl.ds(start, tq), :], m_sc, l_sc, acc_sc)

    step(qi, bias_ref[0, 0])

    @pl.when(qi > 0)
    def _():
        step(qi - 1, bias_ref[0, 1])

    far = far_ref[h]

    def past(c, carry):
        step(c, far)
        return carry

    lax.fori_loop(0, jnp.maximum(qi - 1, 0), past, 0)
    o_ref[...] = (acc_sc[...] / l_sc[...]).astype(o_ref.dtype)


def _moba_prompt(q_maps, kb, vb, means, bias_tiles, far, n_seq, name):
    m = q_maps.shape[0]
    t_len = m // n_seq
    tq = min(ATT_TILE, t_len)
    nq = t_len // tq
    n_candidates = max((t_len - 1) // MOBA_BLOCK, 1)
    n_sel = min(MOBA_TOPK, n_candidates)
    grid_spec = pltpu.PrefetchScalarGridSpec(
        num_scalar_prefetch=1,
        grid=(n_seq, H_MOBA, nq),
        in_specs=[pl.BlockSpec((tq, HEAD_DIM), lambda b, h, i, far: (b * nq + i, H_FOX + h)),
                  pl.BlockSpec((t_len, HEAD_DIM), lambda b, h, i, far: (b, H_FOX + h)),
                  pl.BlockSpec((t_len, HEAD_DIM), lambda b, h, i, far: (b, H_FOX + h)),
                  pl.BlockSpec((1, 1, LANES, HEAD_DIM), lambda b, h, i, far: (b, h, 0, 0)),
                  pl.BlockSpec((1, 2, tq, tq), lambda b, h, i, far: (h, 0, 0, 0))],
        out_specs=pl.BlockSpec((tq, HEAD_DIM), lambda b, h, i, far: (b * nq + i, h)),
        scratch_shapes=_softmax_scratch(tq) + [pltpu.VMEM((tq, t_len), F32)],
    )
    return pl.pallas_call(
        functools.partial(_moba_prompt_body, tq=tq, t_len=t_len, n_candidates=n_candidates, n_sel=n_sel),
        grid_spec=grid_spec,
        out_shape=jax.ShapeDtypeStruct((m, H_MOBA * HEAD_DIM), BF16),
        compiler_params=_cparams(3),
        name=name,
    )(far, q_maps, kb, vb, means, bias_tiles)


def _diff_lambda(lam_ref, lam_init):
    r = lam_ref[...]
    s1 = jnp.sum(r[0:1, :] * r[1:2, :], axis=1, keepdims=True)
    s2 = jnp.sum(r[2:3, :] * r[3:4, :], axis=1, keepdims=True)
    return jnp.exp(s1) - jnp.exp(s2) + lam_init


def _diff_finish(o0, o1, lam, g, lam_init):
    o = o0 - lam * o1
    return o * lax.rsqrt(jnp.mean(o * o, axis=1, keepdims=True) + LN_EPS) * g * (1.0 - lam_init)


def _diff_prompt_body(far_ref, q_ref, k_ref, v_ref, bias_ref, lam_ref, g_ref, o_ref,
                      m0, l0, a0, m1, l1, a1, *, tq, lam_init):
    h = pl.program_id(1)
    qi = pl.program_id(2)
    scale = DIFF_QK_DIM ** -0.5
    q0 = q_ref[:, 0:HEAD_DIM]
    q1 = q_ref[:, HEAD_DIM:2 * HEAD_DIM]
    _init_softmax_state(m0, l0, a0)
    _init_softmax_state(m1, l1, a1)

    def step(c, bias):
        start = pl.multiple_of(c * tq, tq)
        k = k_ref[pl.ds(start, tq), :]
        v = v_ref[pl.ds(start, tq), :]
        _online_softmax_step(_qk(q0, k) * scale + bias, v, m0, l0, a0)
        _online_softmax_step(_qk(q1, k) * scale + bias, v, m1, l1, a1)

    step(qi, bias_ref[0, 0])

    @pl.when(qi > 0)
    def _():
        step(qi - 1, bias_ref[0, 1])

    far = far_ref[H_MOBA + h]

    def past(c, carry):
        step(c, far)
        return carry

    lax.fori_loop(0, jnp.maximum(qi - 1, 0), past, 0)
    lam = _diff_lambda(lam_ref, lam_init)
    o = _diff_finish(a0[...] / l0[...], a1[...] / l1[...], lam, g_ref[...], lam_init)
    o_ref[...] = o.astype(o_ref.dtype)


def _diff_prompt(q_maps, kb, vb, bias_tiles, far, lam_rows, g, lam_init, n_seq, name):
    m = q_maps.shape[0]
    t_len = m // n_seq
    tq = min(ATT_TILE, t_len)
    nq = t_len // tq
    h0 = H_FOX + H_MOBA
    grid_spec = pltpu.PrefetchScalarGridSpec(
        num_scalar_prefetch=1,
        grid=(n_seq, H_DIFF, nq),
        in_specs=[pl.BlockSpec((tq, 2 * HEAD_DIM), lambda b, h, i, far: (b * nq + i, h0 // 2 + h)),
                  pl.BlockSpec((t_len, HEAD_DIM), lambda b, h, i, far: (b, h0 + h)),
                  pl.BlockSpec((t_len, HEAD_DIM), lambda b, h, i, far: (b, h0 + h)),
                  pl.BlockSpec((1, 2, tq, tq), lambda b, h, i, far: (H_MOBA + h, 0, 0, 0)),
                  pl.BlockSpec((8, LANES), lambda b, h, i, far: (0, 0)),
                  pl.BlockSpec((1, HEAD_DIM), lambda b, h, i, far: (0, 0))],
        out_specs=pl.BlockSpec((tq, HEAD_DIM), lambda b, h, i, far: (b * nq + i, h)),
        scratch_shapes=_softmax_scratch(tq) + _softmax_scratch(tq),
    )
    return pl.pallas_call(
        functools.partial(_diff_prompt_body, tq=tq, lam_init=lam_init),
        grid_spec=grid_spec,
        out_shape=jax.ShapeDtypeStruct((m, H_DIFF * HEAD_DIM), BF16),
        compiler_params=_cparams(3),
        name=name,
    )(far, q_maps, kb, vb, bias_tiles, lam_rows, g)


def _sample_page_body(pt_ref, q_ref, *refs):
    g = SAMPLE_PAGES_PER_STEP
    k_refs, v_refs, sfx_refs = refs[0:g], refs[g:2 * g], refs[2 * g:3 * g]
    cb_ref, m_ref, l_ref, acc_ref, ksum_ref = refs[3 * g:]
    n_cols = PAGE_SIZE * N_HEADS
    row = lax.broadcasted_iota(jnp.int32, (SAMPLE_ROWS, n_cols), 0)
    col = lax.broadcasted_iota(jnp.int32, (SAMPLE_ROWS, n_cols), 1)
    scale = jnp.where(row < H_FOX + H_MOBA, HEAD_DIM ** -0.5, DIFF_QK_DIM ** -0.5).astype(F32)
    same_head = (col & (N_HEADS - 1)) == (row & (N_HEADS - 1))
    q = q_ref[0]
    for j in range(g):
        kp = k_refs[j][0, 0]
        kf = kp.reshape(n_cols, HEAD_DIM).astype(BF16)
        vf = v_refs[j][0, 0].reshape(n_cols, HEAD_DIM).astype(BF16)
        s = _qk(q, kf) * scale + (sfx_refs[j][0, 0] + cb_ref[0, j])
        s = jnp.where(same_head, s, NEG_INF)
        m = jnp.max(s, axis=1, keepdims=True)
        p = jnp.exp(s - m)
        l = jnp.sum(p, axis=1, keepdims=True)
        acc_ref[0, j] = jnp.dot(p.astype(BF16), vf, preferred_element_type=F32)
        m_ref[0, j] = jnp.broadcast_to(m, (SAMPLE_ROWS, LANES))
        l_ref[0, j] = jnp.broadcast_to(l, (SAMPLE_ROWS, LANES))
        ksum_ref[0, j] = jnp.sum(kp, axis=0)


def _sample_pages(layer, page_table, q_rows, cache_k, cache_v, sfx_rows, const_rows, name):
    n_seq, n_pages = page_table.shape
    g = SAMPLE_PAGES_PER_STEP
    n_cols = PAGE_SIZE * N_HEADS
    page_block = (1, 1, PAGE_SIZE, N_HEADS, HEAD_DIM)

    def page_spec(j):
        return pl.BlockSpec(page_block, lambda b, p, pt: (layer, pt[b, p * g + j], 0, 0, 0))

    def sfx_spec(j):
        return pl.BlockSpec((1, 1, 1, n_cols), lambda b, p, pt: (0, pt[b, p * g + j], 0, 0))

    stat_spec = pl.BlockSpec((1, g, SAMPLE_ROWS, LANES), lambda b, p, pt: (b, p, 0, 0))
    grid_spec = pltpu.PrefetchScalarGridSpec(
        num_scalar_prefetch=1,
        grid=(n_seq, n_pages // g),
        in_specs=([pl.BlockSpec((1, SAMPLE_ROWS, HEAD_DIM), lambda b, p, pt: (b, 0, 0))]
                  + [page_spec(j) for j in range(g)] + [page_spec(j) for j in range(g)]
                  + [sfx_spec(j) for j in range(g)]
                  + [pl.BlockSpec((1, g, 1, n_cols), lambda b, p, pt: (b, p, 0, 0))]),
        out_specs=[stat_spec, stat_spec, stat_spec,
                   pl.BlockSpec((1, g, N_HEADS, HEAD_DIM), lambda b, p, pt: (b, p, 0, 0))],
    )
    stat_shape = jax.ShapeDtypeStruct((n_seq, n_pages, SAMPLE_ROWS, LANES), F32)
    return pl.pallas_call(
        _sample_page_body,
        grid_spec=grid_spec,
        out_shape=[stat_shape, stat_shape, stat_shape,
                   jax.ShapeDtypeStruct((n_seq, n_pages, N_HEADS, HEAD_DIM), F32)],
        compiler_params=_cparams(2),
        name=name,
    )(page_table, q_rows, *([cache_k] * g), *([cache_v] * g), *([sfx_rows] * g), const_rows)


def _sample_combine_body(m_ref, l_ref, acc_ref, ksum_ref, q_ref, k_ref, v_ref, sb_ref, lam_ref, g_ref, o_ref,
                         *, n_pages, lam_init):
    n_blocks = n_pages * PAGE_SIZE // MOBA_BLOCK
    pages_per_block = MOBA_BLOCK // PAGE_SIZE
    n_sel = min(MOBA_TOPK, n_blocks)
    m = m_ref[0]
    l = l_ref[0]
    acc = acc_ref[0]
    q = q_ref[0]
    head = lax.broadcasted_iota(jnp.int32, (N_HEADS, LANES), 0)
    is_moba = jnp.where(head >= H_FOX, jnp.where(head < H_FOX + H_MOBA, 1.0, 0.0), 0.0)

    gp = jnp.sum(ksum_ref[0] * q[0:N_HEADS][None], axis=2, keepdims=True)
    gb = jnp.sum(gp.reshape(n_blocks, pages_per_block, N_HEADS, 1), axis=1) * (1.0 / MOBA_BLOCK)
    gb = jnp.broadcast_to(gb, (n_blocks, N_HEADS, LANES))
    blk = lax.broadcasted_iota(jnp.int32, (n_blocks, N_HEADS, LANES), 0)
    rank = jnp.zeros((n_blocks, N_HEADS, LANES), F32)
    for c2 in range(n_blocks):
        g2 = gb[c2:c2 + 1]
        rank = rank + jnp.where(g2 > gb, 1.0, jnp.where(g2 == gb, jnp.where(blk > c2, 1.0, 0.0), 0.0))
    sel_b = jnp.where(rank < n_sel, 1.0, 0.0)
    sel_b = jnp.maximum(sel_b, 1.0 - is_moba[None])
    sel_p = jnp.broadcast_to(sel_b[:, None], (n_blocks, pages_per_block, N_HEADS, LANES))
    sel_p = sel_p.reshape(n_pages, N_HEADS, LANES)
    sel = jnp.concatenate([sel_p, jnp.ones_like(sel_p)], axis=1)

    row = lax.broadcasted_iota(jnp.int32, (SAMPLE_ROWS, 1), 0)
    scale = jnp.where(row < H_FOX + H_MOBA, HEAD_DIM ** -0.5, DIFF_QK_DIM ** -0.5).astype(F32)
    s_self = jnp.sum(q * k_ref[0], axis=1, keepdims=True) * scale + sb_ref[0]

    m_sel = jnp.where(sel > 0.0, m, NEG_INF)
    m_all = jnp.maximum(jnp.max(m_sel, axis=0), s_self)
    w = jnp.where(sel > 0.0, jnp.exp(m_sel - m_all[None]), 0.0)
    w_self = jnp.exp(s_self - m_all)
    den = jnp.sum(w * l, axis=0) + w_self
    num = jnp.sum(w * acc, axis=0) + w_self * v_ref[0]
    o = num / den
    o_first, o_second = o[0:N_HEADS], o[N_HEADS:SAMPLE_ROWS]
    lam = _diff_lambda(lam_ref, lam_init)
    od = _diff_finish(o_first, o_second, lam, g_ref[...], lam_init)
    o_ref[0] = jnp.where(head >= H_FOX + H_MOBA, od, o_first).astype(o_ref.dtype)


def _sample_combine(m, l, acc, ksum, q_rows32, k_rows, v_rows, self_bias, lam_rows, g, lam_init, name):
    n_seq, n_pages = m.shape[:2]
    stat_spec = pl.BlockSpec((1, n_pages, SAMPLE_ROWS, LANES), lambda b: (b, 0, 0, 0))
    row_spec = pl.BlockSpec((1, SAMPLE_ROWS, HEAD_DIM), lambda b: (b, 0, 0))
    return pl.pallas_call(
        functools.partial(_sample_combine_body, n_pages=n_pages, lam_init=lam_init),
        grid=(n_seq,),
        in_specs=[stat_spec, stat_spec, stat_spec,
                  pl.BlockSpec((1, n_pages, N_HEADS, HEAD_DIM), lambda b: (b, 0, 0, 0)),
                  row_spec, row_spec, row_spec,
                  pl.BlockSpec((1, SAMPLE_ROWS, LANES), lambda b: (0, 0, 0)),
                  pl.BlockSpec((8, LANES), lambda b: (0, 0)),
                  pl.BlockSpec((1, HEAD_DIM), lambda b: (0, 0))],
        out_specs=pl.BlockSpec((1, N_HEADS, HEAD_DIM), lambda b: (b, 0, 0)),
        out_shape=jax.ShapeDtypeStruct((n_seq, N_HEADS, HEAD_DIM), BF16),
        compiler_params=_cparams(1),
        name=name,
    )(m, l, acc, ksum, q_rows32, k_rows, v_rows, self_bias, lam_rows, g)


def _router_body(x_ref, w_ref, b_ref, o_ref):
    logits = jnp.dot(x_ref[...], w_ref[...], precision=HIGHEST, preferred_element_type=F32) + b_ref[...]
    lane = lax.broadcasted_iota(jnp.int32, logits.shape, 1)
    lane_f = lane.astype(F32)
    logits = jnp.where(lane < N_EXPERTS, logits, NEG_INF)
    v1 = jnp.max(logits, axis=1, keepdims=True)
    i1 = jnp.min(jnp.where(logits == v1, lane_f, float(LANES)), axis=1, keepdims=True)
    rest = jnp.where(lane_f == i1, NEG_INF, logits)
    v2 = jnp.max(rest, axis=1, keepdims=True)
    i2 = jnp.min(jnp.where(rest == v2, lane_f, float(LANES)), axis=1, keepdims=True)
    e2 = jnp.exp(v2 - v1)
    g1 = 1.0 / (1.0 + e2)
    g2 = e2 / (1.0 + e2)
    out = jnp.where(lane == 0, i1, jnp.where(lane == 1, i2, jnp.where(lane == 2, g1, jnp.where(lane == 3, g2, 0.0))))
    o_ref[...] = out


def _router(x32, w_r, b_r, tm, name):
    m = x32.shape[0]
    tm = min(tm, m)
    return pl.pallas_call(
        _router_body,
        grid=(m // tm,),
        in_specs=[pl.BlockSpec((tm, D_MODEL), lambda i: (i, 0)),
                  pl.BlockSpec((D_MODEL, LANES), lambda i: (0, 0)),
                  pl.BlockSpec((1, LANES), lambda i: (0, 0))],
        out_specs=pl.BlockSpec((tm, LANES), lambda i: (i, 0)),
        out_shape=jax.ShapeDtypeStruct((m, LANES), F32),
        compiler_params=_cparams(1),
        name=name,
    )(x32, w_r, b_r)


def _row_copy(src_hbm, idx, dst, r, sem):
    return pltpu.make_async_copy(src_hbm.at[pl.ds(idx, 1), :], dst.at[pl.ds(r, 1), :], sem)


def _gather_rows(src_hbm, idx_ref, dst, sem, n_rows):
    def start(r0, carry):
        for u in range(GATHER_UNROLL):
            r = r0 * GATHER_UNROLL + u
            _row_copy(src_hbm, idx_ref[0, 0, r], dst, r, sem).start(priority=u % 2)
        return carry

    lax.fori_loop(0, n_rows // GATHER_UNROLL, start, 0)
    pltpu.make_async_copy(src_hbm.at[pl.ds(0, n_rows), :], dst, sem).wait()


def _dispatch_body(idx_ref, x_hbm, o_ref, buf, sem, *, tr):
    _gather_rows(x_hbm, idx_ref, buf, sem, tr)
    o_ref[...] = buf[...].astype(o_ref.dtype)


def _dispatch(x32, row_token, tr, name):
    p = row_token.shape[0]
    idx = row_token.reshape(p // tr, 1, tr)
    return pl.pallas_call(
        functools.partial(_dispatch_body, tr=tr),
        grid=(p // tr,),
        in_specs=[pl.BlockSpec((1, 1, tr), lambda i: (i, 0, 0), memory_space=pltpu.SMEM),
                  pl.BlockSpec(memory_space=pl.ANY)],
        out_specs=pl.BlockSpec((tr, D_MODEL), lambda i: (i, 0)),
        out_shape=jax.ShapeDtypeStruct((p, D_MODEL), BF16),
        scratch_shapes=[pltpu.VMEM((tr, D_MODEL), F32), pltpu.SemaphoreType.DMA(())],
        compiler_params=_cparams(1),
        name=name,
    )(idx, x32)


def _expert_up_body(te_ref, nu_ref, x_ref, wg_ref, wu_ref, h_ref):
    @pl.when(pl.program_id(0) < nu_ref[0])
    def _():
        x = x_ref[...]
        g = jnp.dot(x, wg_ref[0], preferred_element_type=F32)
        u = jnp.dot(x, wu_ref[0], preferred_element_type=F32)
        h_ref[...] = (g * (1.0 / (1.0 + jnp.exp(-g))) * u).astype(h_ref.dtype)

    @pl.when(pl.program_id(0) >= nu_ref[0])
    def _():
        h_ref[...] = jnp.zeros_like(h_ref)


def _expert_up(xs, wg, wu, tile_expert, n_used, tm, tn, name):
    p, k = xs.shape
    f = wg.shape[2]
    tn = min(tn, f)
    nf = f // tn
    wspec = pl.BlockSpec((1, k, tn), lambda i, j, te, nu: (te[i], 0, jnp.where(i < nu[0], j, nf - 1)))
    grid_spec = pltpu.PrefetchScalarGridSpec(
        num_scalar_prefetch=2,
        grid=(p // tm, nf),
        in_specs=[pl.BlockSpec((tm, k), lambda i, j, te, nu: (i, 0)), wspec, wspec],
        out_specs=pl.BlockSpec((tm, tn), lambda i, j, te, nu: (i, j)),
    )
    return pl.pallas_call(
        _expert_up_body,
        grid_spec=grid_spec,
        out_shape=jax.ShapeDtypeStruct((p, f), BF16),
        compiler_params=_cparams(2),
        name=name,
    )(tile_expert, n_used, xs, wg, wu)


def _expert_down_body(te_ref, nu_ref, h_ref, w_ref, y_ref, acc):
    kk = pl.program_id(1)

    @pl.when(pl.program_id(0) < nu_ref[0])
    def _():
        @pl.when(kk == 0)
        def _():
            acc[...] = jnp.zeros_like(acc)

        acc[...] += jnp.dot(h_ref[...], w_ref[0], preferred_element_type=F32)

        @pl.when(kk == pl.num_programs(1) - 1)
        def _():
            y_ref[...] = acc[...]

    @pl.when(pl.program_id(0) >= nu_ref[0])
    def _():
        y_ref[...] = jnp.zeros_like(y_ref)


def _expert_down(h, wd, tile_expert, n_used, tm, tk, name):
    p, f = h.shape
    tk = min(tk, f)
    nk = f // tk
    grid_spec = pltpu.PrefetchScalarGridSpec(
        num_scalar_prefetch=2,
        grid=(p // tm, nk),
        in_specs=[pl.BlockSpec((tm, tk), lambda i, k, te, nu: (i, k)),
                  pl.BlockSpec((1, tk, D_MODEL), lambda i, k, te, nu: (te[i], jnp.where(i < nu[0], k, nk - 1), 0))],
        out_specs=pl.BlockSpec((tm, D_MODEL), lambda i, k, te, nu: (i, 0)),
        scratch_shapes=[pltpu.VMEM((tm, D_MODEL), F32)],
    )
    return pl.pallas_call(
        _expert_down_body,
        grid_spec=grid_spec,
        out_shape=jax.ShapeDtypeStruct((p, D_MODEL), F32),
        compiler_params=_cparams(2),
        name=name,
    )(tile_expert, n_used, h, wd)


def _moe_combine_body(p0_ref, p1_ref, y_hbm, r_ref, x_ref, g_ref, b_ref, o32_ref, ob_ref, buf0, buf1, sem0, sem1,
                      *, tr):
    _gather_rows(y_hbm, p0_ref, buf0, sem0, tr)
    _gather_rows(y_hbm, p1_ref, buf1, sem1, tr)
    r = r_ref[...]
    y = ALPHA * x_ref[...] + (r[:, 2:3] * buf0[...] + r[:, 3:4] * buf1[...])
    y = _layer_norm_rows(y, g_ref[...], b_ref[...])
    o32_ref[...] = y
    ob_ref[...] = y.astype(BF16)


def _moe_combine(y_sorted, pos0, pos1, routes, x32, g, b, tr, name):
    m = x32.shape[0]
    tr = min(tr, m)
    idx_spec = pl.BlockSpec((1, 1, tr), lambda i: (i, 0, 0), memory_space=pltpu.SMEM)
    row = lambda i: (i, 0)
    fixed = lambda i: (0, 0)
    return pl.pallas_call(
        functools.partial(_moe_combine_body, tr=tr),
        grid=(m // tr,),
        in_specs=[idx_spec, idx_spec,
                  pl.BlockSpec(memory_space=pl.ANY),
                  pl.BlockSpec((tr, LANES), row),
                  pl.BlockSpec((tr, D_MODEL), row),
                  pl.BlockSpec((1, D_MODEL), fixed),
                  pl.BlockSpec((1, D_MODEL), fixed)],
        out_specs=[pl.BlockSpec((tr, D_MODEL), row)] * 2,
        out_shape=[jax.ShapeDtypeStruct((m, D_MODEL), F32), jax.ShapeDtypeStruct((m, D_MODEL), BF16)],
        scratch_shapes=[pltpu.VMEM((tr, D_MODEL), F32), pltpu.VMEM((tr, D_MODEL), F32),
                        pltpu.SemaphoreType.DMA(()), pltpu.SemaphoreType.DMA(())],
        compiler_params=_cparams(1),
        name=name,
    )(pos0.reshape(m // tr, 1, tr), pos1.reshape(m // tr, 1, tr), y_sorted, routes, x32, g, b)


def _routing_plan(routes, tm):
    m = routes.shape[0]
    experts = jnp.concatenate([routes[:, 0], routes[:, 1]]).astype(jnp.int32)
    onehot = (experts[:, None] == jnp.arange(N_EXPERTS, dtype=jnp.int32)[None, :]).astype(jnp.int32)
    csum = jnp.cumsum(onehot, axis=0)
    rank = jnp.sum(csum * onehot, axis=1) - 1
    counts = csum[-1]
    padded = ((counts + tm - 1) // tm) * tm
    ends = jnp.cumsum(padded)
    starts = ends - padded
    pos = jnp.sum(onehot * starts[None, :], axis=1) + rank
    n_rows = (pl.cdiv(TOP_K * m, tm) + N_EXPERTS) * tm
    token = jnp.concatenate([jnp.arange(m, dtype=jnp.int32)] * TOP_K)
    row_token = jnp.zeros((n_rows,), jnp.int32).at[pos].set(token)
    tile_start = jnp.arange(n_rows // tm, dtype=jnp.int32) * tm
    tile_expert = jnp.minimum(jnp.sum((tile_start[:, None] >= ends[None, :]).astype(jnp.int32), axis=1),
                              N_EXPERTS - 1)
    n_used = (ends[-1] // tm).reshape(1).astype(jnp.int32)
    return row_token, pos[:m], pos[m:], tile_expert, n_used


def _moe_layer(x32, w_r, b_r, wg, wu, wd, g, b, tm, tn, tk, tag):
    routes = _router(x32, w_r, b_r, 512, f"router_{tag}")
    row_token, pos0, pos1, tile_expert, n_used = _routing_plan(routes, tm)
    xs = _dispatch(x32, row_token, tm, f"moe_dispatch_{tag}")
    h = _expert_up(xs, wg, wu, tile_expert, n_used, tm, tn, f"moe_up_{tag}")
    y = _expert_down(h, wd, tile_expert, n_used, tm, tk, f"moe_down_{tag}")
    return _moe_combine(y, pos0, pos1, routes, x32, g, b, 256, f"moe_combine_{tag}")


def _q_map_weights(w_q):
    h0 = (H_FOX + H_MOBA) * HEAD_DIM
    cols = [w_q[:, :h0]]
    zeros = jnp.zeros((w_q.shape[0], DIFF_QK_DIM), w_q.dtype)
    for i in range(H_DIFF):
        wh = w_q[:, h0 + i * HEAD_DIM:h0 + (i + 1) * HEAD_DIM]
        cols += [wh[:, :DIFF_QK_DIM], zeros, zeros, wh[:, DIFF_QK_DIM:]]
    return jnp.concatenate(cols, axis=1)


def _sample_row_order():
    first = list(range(H_FOX + H_MOBA)) + [H_FOX + H_MOBA + 2 * i for i in range(H_DIFF)]
    second = [H_FOX + H_MOBA + 2 * i + 1 for i in range(H_DIFF)]
    return first, second


def kernel(x_prompt, x_sample, cache_k, cache_v, cache_logf, page_table, w_in, b_forget, w_out, lam_q1, lam_k1, lam_q2, lam_k2, subln_g, rel_bias, ln1_g, ln1_b, ln2_g, ln2_b, w_ffn_gate, w_ffn_up, w_ffn_down, w_router, b_router, w_exp_gate, w_exp_up, w_exp_down):
    n_seq, t_len, _ = x_prompt.shape
    n_dec, dec_seq, _ = x_sample.shape
    assert dec_seq == 1 and t_len % MOBA_BLOCK == 0
    n_pages = page_table.shape[1]
    past = n_pages * PAGE_SIZE
    assert past % MOBA_BLOCK == 0 and past >= MOBA_BLOCK and n_pages % SAMPLE_PAGES_PER_STEP == 0
    depth = w_in.shape[0]
    n_pool = cache_k.shape[1]
    mp, ms = n_seq * t_len, n_dec
    n_cols = PAGE_SIZE * N_HEADS

    rel_t = jnp.transpose(rel_bias).astype(F32)
    far = rel_t[:, N_BUCKETS - 1]
    bias_tiles = _bias_tiles(rel_t, min(ATT_TILE, t_len))
    dist = past - (jnp.arange(n_pages)[:, None] * PAGE_SIZE + jnp.arange(PAGE_SIZE)[None, :])
    bucket_of = jnp.array([_t5_bucket_static(d) for d in range(T5_FAR + 1)], jnp.int32)
    bkt = bucket_of[jnp.minimum(dist, T5_FAR)]
    t5_cols = jnp.concatenate([jnp.zeros((n_pages, PAGE_SIZE, H_FOX), F32), rel_bias[bkt]], axis=2)
    first_rows, second_rows = _sample_row_order()
    self_bias16 = jnp.concatenate([jnp.zeros((H_FOX,), F32), rel_bias[0]])
    self_bias = jnp.broadcast_to(jnp.concatenate([self_bias16, self_bias16])[None, :, None],
                                 (1, SAMPLE_ROWS, LANES))

    xp32 = x_prompt.reshape(mp, D_MODEL)
    xs32 = x_sample.reshape(ms, D_MODEL)
    xpb, xsb = xp32.astype(BF16), xs32.astype(BF16)
    outs = {k: [] for k in ("kp", "vp", "fp", "ks", "vs", "fs")}

    for l in range(depth):
        lam_init = 0.8 - 0.6 * math.exp(-0.3 * l)
        w_q = _q_map_weights(w_in[l, :, :D_MODEL]).astype(BF16)
        w_k = w_in[l, :, D_MODEL:2 * D_MODEL].astype(BF16)
        w_v = w_in[l, :, 2 * D_MODEL:3 * D_MODEL].astype(BF16)
        w_f = jnp.pad(w_in[l, :, 3 * D_MODEL:], ((0, 0), (0, LANES - H_FOX)))
        b_f = jnp.pad(b_forget[l], (0, LANES - H_FOX)).reshape(1, LANES)
        w_o = w_out[l].astype(BF16)
        lam_rows = jnp.pad(jnp.stack([lam_q1[l], lam_k1[l], lam_q2[l], lam_k2[l]]),
                           ((0, 4), (0, LANES - DIFF_QK_DIM)))
        g_sub = subln_g[l].reshape(1, HEAD_DIM)
        g1, b1 = ln1_g[l].reshape(1, D_MODEL), ln1_b[l].reshape(1, D_MODEL)
        g2, b2 = ln2_g[l].reshape(1, D_MODEL), ln2_b[l].reshape(1, D_MODEL)

        (qp,) = _matmul(xpb, w_q, [BF16], 512, 512, f"proj_q_p{l}")
        kp32, kpb = _matmul(xpb, w_k, [F32, BF16], 512, 512, f"proj_k_p{l}")
        vp32, vpb = _matmul(xpb, w_v, [F32, BF16], 512, 512, f"proj_v_p{l}")
        lfp, fcum = _log_forget(xp32, w_f, b_f, n_seq, True, f"logf_p{l}")
        f_cols = jnp.transpose(fcum.reshape(n_seq, t_len, LANES)[:, :, :8], (0, 2, 1)).reshape(n_seq, 8, 1, t_len)
        o_fox = _fox_prompt(qp, kpb, vpb, fcum, f_cols, n_seq, f"fox_p{l}")
        means = _block_means(kp32, n_seq, f"moba_means_p{l}")
        o_moba = _moba_prompt(qp, kpb, vpb, means, bias_tiles, far, n_seq, f"moba_p{l}")
        o_diff = _diff_prompt(qp, kpb, vpb, bias_tiles, far, lam_rows, g_sub, lam_init, n_seq, f"diff_p{l}")
        xp32, xpb = _outproj_ln(o_fox, o_moba, o_diff, w_o, xp32, g1, b1, 256, f"outproj_p{l}")

        (qs,) = _matmul(xsb, w_q, [F32], 512, 512, f"proj_q_s{l}")
        ks32, _ = _matmul(xsb, w_k, [F32, BF16], 512, 512, f"proj_k_s{l}")
        vs32, _ = _matmul(xsb, w_v, [F32, BF16], 512, 512, f"proj_v_s{l}")
        lfs, _ = _log_forget(xs32, w_f, b_f, n_dec, False, f"logf_s{l}")
        qs = qs.reshape(ms, N_QMAPS, HEAD_DIM)
        pad_rows = jnp.zeros((ms, SAMPLE_ROWS - N_HEADS - H_DIFF, HEAD_DIM), F32)
        q_rows32 = jnp.concatenate([qs[:, jnp.array(first_rows)], pad_rows, qs[:, jnp.array(second_rows)]], axis=1)
        k_heads = ks32.reshape(ms, N_HEADS, HEAD_DIM)
        v_heads = vs32.reshape(ms, N_HEADS, HEAD_DIM)
        k_rows = jnp.concatenate([k_heads, k_heads], axis=1)
        v_rows = jnp.concatenate([v_heads, v_heads], axis=1)
        lf_pool = cache_logf[l].astype(F32)
        pool_tot = jnp.sum(lf_pool, axis=1)
        sfx_pool = pool_tot[:, None, :] - jnp.cumsum(lf_pool, axis=1)
        sfx_rows = jnp.pad(sfx_pool, ((0, 0), (0, 0), (0, N_HEADS - H_FOX))).reshape(1, n_pool, 1, n_cols)
        page_tot = pool_tot[page_table]
        later = jnp.sum(page_tot, axis=1, keepdims=True) - jnp.cumsum(page_tot, axis=1)
        fox_const = later + lfs[:, None, :H_FOX]
        const16 = jnp.concatenate(
            [jnp.broadcast_to(fox_const[:, :, None, :], (ms, n_pages, PAGE_SIZE, H_FOX)),
             jnp.broadcast_to(t5_cols[None, :, :, H_FOX:], (ms, n_pages, PAGE_SIZE, N_HEADS - H_FOX))], axis=3)
        const_rows = const16.reshape(ms, n_pages, 1, n_cols)
        m_s, l_s, acc_s, ksum = _sample_pages(l, page_table, q_rows32.astype(BF16), cache_k, cache_v,
                                              sfx_rows, const_rows, f"sample_pages{l}")
        o_s = _sample_combine(m_s, l_s, acc_s, ksum, q_rows32, k_rows, v_rows, self_bias, lam_rows, g_sub,
                              lam_init, f"sample_combine{l}").reshape(ms, D_MODEL)
        nf, nm = H_FOX * HEAD_DIM, (H_FOX + H_MOBA) * HEAD_DIM
        xs32, xsb = _outproj_ln(o_s[:, :nf], o_s[:, nf:nm], o_s[:, nm:], w_o, xs32, g1, b1, 256, f"outproj_s{l}")

        i = l // 2
        if l % 2 == 0:
            wg, wu, wd = (w_ffn_gate[i].astype(BF16), w_ffn_up[i].astype(BF16), w_ffn_down[i].astype(BF16))
            hp = _swiglu_up(xpb, wg, wu, 1024, 512, f"ffn_up_p{l}")
            xp32, xpb = _down_ln(hp, wd, xp32, g2, b2, 512, 1408, f"ffn_down_p{l}")
            hs = _swiglu_up(xsb, wg, wu, 1024, 512, f"ffn_up_s{l}")
            xs32, xsb = _down_ln(hs, wd, xs32, g2, b2, 512, 1408, f"ffn_down_s{l}")
        else:
            wg, wu, wd = (w_exp_gate[i].astype(BF16), w_exp_up[i].astype(BF16), w_exp_down[i].astype(BF16))
            w_r = jnp.pad(w_router[i], ((0, 0), (0, LANES - N_EXPERTS)))
            b_r = jnp.pad(b_router[i], (0, LANES - N_EXPERTS)).reshape(1, LANES)
            xp32, xpb = _moe_layer(xp32, w_r, b_r, wg, wu, wd, g2, b2, 512, 512, 1408, f"p{l}")
            xs32, xsb = _moe_layer(xs32, w_r, b_r, wg, wu, wd, g2, b2, 16, 512, 1408, f"s{l}")

        outs["kp"].append(kp32.reshape(n_seq, t_len // PAGE_SIZE, PAGE_SIZE, N_HEADS, HEAD_DIM))
        outs["vp"].append(vp32.reshape(n_seq, t_len // PAGE_SIZE, PAGE_SIZE, N_HEADS, HEAD_DIM))
        outs["fp"].append(lfp[:, :H_FOX].reshape(n_seq, t_len // PAGE_SIZE, PAGE_SIZE, H_FOX))
        outs["ks"].append(ks32.reshape(n_dec, 1, N_HEADS, HEAD_DIM))
        outs["vs"].append(vs32.reshape(n_dec, 1, N_HEADS, HEAD_DIM))
        outs["fs"].append(lfs[:, :H_FOX].reshape(n_dec, 1, H_FOX))

    return (xp32.reshape(n_seq, t_len, D_MODEL), xs32.reshape(n_dec, 1, D_MODEL),
            jnp.stack(outs["kp"]), jnp.stack(outs["vp"]), jnp.stack(outs["fp"]),
            jnp.stack(outs["ks"]), jnp.stack(outs["vs"]), jnp.stack(outs["fs"]))
```

```python
import functools
import math

import jax
import jax.numpy as jnp
from jax import lax
from jax.experimental import pallas as pl
from jax.experimental.pallas import tpu as pltpu

F32 = jnp.float32
BF16 = jnp.bfloat16
HIGHEST = lax.Precision.HIGHEST
NEG_INF = float("-inf")

HEAD_DIM = 128
N_HEADS = 16
D_MODEL = N_HEADS * HEAD_DIM
H_FOX = 6
H_MOBA = 6
H_DIFF = 4
DIFF_QK_DIM = HEAD_DIM // 2
N_QMAPS = H_FOX + H_MOBA + 2 * H_DIFF
MOBA_BLOCK = 256
MOBA_TOPK = 3
PAGE_SIZE = 128
N_BUCKETS = 32
MAX_DISTANCE = 128
N_EXPERTS = 8
TOP_K = 2
DEPTH = 2
ALPHA = (2.0 * DEPTH) ** 0.25
LN_EPS = 1e-5

LANES = 128
VMEM_LIMIT_BYTES = 56 * 1024 * 1024
ATT_TILE = 2 * MOBA_BLOCK
LOGF_TILE = 256
SAMPLE_ROWS = 2 * N_HEADS
SAMPLE_PAGES_PER_STEP = 4
SLAB = D_MODEL // LANES
GATHER_UNROLL = 8


def _cparams(n_axes):
    return pltpu.CompilerParams(
        dimension_semantics=("arbitrary",) * n_axes, vmem_limit_bytes=VMEM_LIMIT_BYTES)


def _t5_thresholds():
    max_exact = N_BUCKETS // 2

    def bucket(d):
        if d < max_exact:
            return d
        r = math.log(d / max_exact) / math.log(MAX_DISTANCE / max_exact)
        return min(max_exact + int(r * (N_BUCKETS - max_exact)), N_BUCKETS - 1)

    return [min(d for d in range(2 * MAX_DISTANCE) if bucket(d) >= b) for b in range(N_BUCKETS)]


T5_THRESHOLDS = _t5_thresholds()
T5_FAR = T5_THRESHOLDS[-1]


def _t5_bucket_static(d):
    b = 0
    for i, t in enumerate(T5_THRESHOLDS):
        if d >= t:
            b = i
    return b


def _mm_body(x_ref, w_ref, *o_refs):
    acc = jnp.dot(x_ref[...], w_ref[...], preferred_element_type=F32)
    for o in o_refs:
        o[...] = acc.astype(o.dtype)


def _matmul(x, w, out_dtypes, tm, tn, name):
    m, k = x.shape
    n = w.shape[1]
    tm, tn = min(tm, m), min(tn, n)
    return pl.pallas_call(
        _mm_body,
        grid=(m // tm, n // tn),
        in_specs=[pl.BlockSpec((tm, k), lambda i, j: (i, 0)),
                  pl.BlockSpec((k, tn), lambda i, j: (0, j))],
        out_specs=[pl.BlockSpec((tm, tn), lambda i, j: (i, j)) for _ in out_dtypes],
        out_shape=[jax.ShapeDtypeStruct((m, n), d) for d in out_dtypes],
        compiler_params=_cparams(2),
        name=name,
    )(x, w)


def _log_sigmoid(z):
    return jnp.minimum(z, 0.0) - jnp.log(1.0 + jnp.exp(-jnp.abs(z)))


def _logf_body(x_ref, w_ref, b_ref, lf_ref, cum_ref, carry, *, tt, cumulative):
    t = pl.program_id(1)
    z = jnp.dot(x_ref[...], w_ref[...], precision=HIGHEST, preferred_element_type=F32) + b_ref[...]
    lf = _log_sigmoid(z)
    lf_ref[...] = lf
    if cumulative:
        @pl.when(t == 0)
        def _():
            carry[...] = jnp.zeros_like(carry)

        row = lax.broadcasted_iota(jnp.int32, (tt, tt), 0)
        col = lax.broadcasted_iota(jnp.int32, (tt, tt), 1)
        tri = jnp.where(col <= row, 1.0, 0.0).astype(F32)
        cum = jnp.dot(tri, lf, precision=HIGHEST, preferred_element_type=F32) + carry[...]
        cum_ref[...] = cum
        carry[...] = cum[tt - 1:tt, :]
    else:
        cum_ref[...] = lf


def _log_forget(x32, w_f, b_f, n_seq, cumulative, name):
    m = x32.shape[0]
    t_len = m // n_seq
    tt = min(LOGF_TILE, t_len) if cumulative else m
    n_seq_grid = n_seq if cumulative else 1
    nt = (m // n_seq_grid) // tt
    spec = pl.BlockSpec((tt, LANES), lambda b, t: (b * nt + t, 0))
    return pl.pallas_call(
        functools.partial(_logf_body, tt=tt, cumulative=cumulative),
        grid=(n_seq_grid, nt),
        in_specs=[pl.BlockSpec((tt, D_MODEL), lambda b, t: (b * nt + t, 0)),
                  pl.BlockSpec((D_MODEL, LANES), lambda b, t: (0, 0)),
                  pl.BlockSpec((1, LANES), lambda b, t: (0, 0))],
        out_specs=[spec, spec],
        out_shape=[jax.ShapeDtypeStruct((m, LANES), F32)] * 2,
        scratch_shapes=[pltpu.VMEM((1, LANES), F32)],
        compiler_params=_cparams(2),
        name=name,
    )(x32, w_f, b_f)


def _layer_norm_rows(y, g, b):
    mean = jnp.mean(y, axis=1, keepdims=True)
    yc = y - mean
    var = jnp.mean(yc * yc, axis=1, keepdims=True)
    return yc * lax.rsqrt(var + LN_EPS) * g + b


def _outproj_ln_body(of_ref, om_ref, od_ref, w_ref, x_ref, g_ref, b_ref, o32_ref, ob_ref, *slab_refs):
    nf, nm = H_FOX * HEAD_DIM, (H_FOX + H_MOBA) * HEAD_DIM
    a = jnp.dot(of_ref[...], w_ref[0:nf, :], preferred_element_type=F32)
    a += jnp.dot(om_ref[...], w_ref[nf:nm, :], preferred_element_type=F32)
    a += jnp.dot(od_ref[...], w_ref[nm:D_MODEL, :], preferred_element_type=F32)
    y = _layer_norm_rows(ALPHA * x_ref[...] + a, g_ref[...], b_ref[...])
    o32_ref[...] = y
    ob_ref[...] = y.astype(BF16)
    for slab_ref in slab_refs:
        _store_as_slabs(slab_ref, y)


def _outproj_ln(o_fox, o_moba, o_diff, w_out, x32, g, b, tm, emit_slabs, name):
    m = x32.shape[0]
    tm = min(tm, m)
    row = lambda i: (i, 0)
    fixed = lambda i: (0, 0)
    out_specs = [pl.BlockSpec((tm, D_MODEL), row)] * 2
    out_shape = [jax.ShapeDtypeStruct((m, D_MODEL), F32), jax.ShapeDtypeStruct((m, D_MODEL), BF16)]
    if emit_slabs:
        out_specs.append(pl.BlockSpec((tm * SLAB, LANES), row))
        out_shape.append(jax.ShapeDtypeStruct((m * SLAB, LANES), F32))
    return pl.pallas_call(
        _outproj_ln_body,
        grid=(m // tm,),
        in_specs=[pl.BlockSpec((tm, o_fox.shape[1]), row),
                  pl.BlockSpec((tm, o_moba.shape[1]), row),
                  pl.BlockSpec((tm, o_diff.shape[1]), row),
                  pl.BlockSpec((D_MODEL, D_MODEL), fixed, pipeline_mode=pl.Buffered(1)),
                  pl.BlockSpec((tm, D_MODEL), row),
                  pl.BlockSpec((1, D_MODEL), fixed),
                  pl.BlockSpec((1, D_MODEL), fixed)],
        out_specs=out_specs,
        out_shape=out_shape,
        compiler_params=_cparams(1),
        name=name,
    )(o_fox, o_moba, o_diff, w_out, x32, g, b)


def _swiglu_up_body(x_ref, wg_ref, wu_ref, h_ref):
    x = x_ref[...]
    g = jnp.dot(x, wg_ref[...], preferred_element_type=F32)
    u = jnp.dot(x, wu_ref[...], preferred_element_type=F32)
    h_ref[...] = (g * (1.0 / (1.0 + jnp.exp(-g))) * u).astype(h_ref.dtype)


def _swiglu_up(xb, wg, wu, tm, tn, name):
    m, k = xb.shape
    f = wg.shape[1]
    tm, tn = min(tm, m), min(tn, f)
    return pl.pallas_call(
        _swiglu_up_body,
        grid=(m // tm, f // tn),
        in_specs=[pl.BlockSpec((tm, k), lambda i, j: (i, 0)),
                  pl.BlockSpec((k, tn), lambda i, j: (0, j)),
                  pl.BlockSpec((k, tn), lambda i, j: (0, j))],
        out_specs=pl.BlockSpec((tm, tn), lambda i, j: (i, j)),
        out_shape=jax.ShapeDtypeStruct((m, f), BF16),
        compiler_params=_cparams(2),
        name=name,
    )(xb, wg, wu)


def _down_ln_body(h_ref, w_ref, x_ref, g_ref, b_ref, o32_ref, ob_ref, acc):
    kk = pl.program_id(1)

    @pl.when(kk == 0)
    def _():
        acc[...] = jnp.zeros_like(acc)

    acc[...] += jnp.dot(h_ref[...], w_ref[...], preferred_element_type=F32)

    @pl.when(kk == pl.num_programs(1) - 1)
    def _():
        y = _layer_norm_rows(ALPHA * x_ref[...] + acc[...], g_ref[...], b_ref[...])
        o32_ref[...] = y
        ob_ref[...] = y.astype(BF16)


def _down_ln(h, wd, x32, g, b, tm, tk, name):
    m, f = h.shape
    tm, tk = min(tm, m), min(tk, f)
    row = lambda i, k: (i, 0)
    fixed = lambda i, k: (0, 0)
    return pl.pallas_call(
        _down_ln_body,
        grid=(m // tm, f // tk),
        in_specs=[pl.BlockSpec((tm, tk), lambda i, k: (i, k)),
                  pl.BlockSpec((tk, D_MODEL), lambda i, k: (k, 0)),
                  pl.BlockSpec((tm, D_MODEL), row),
                  pl.BlockSpec((1, D_MODEL), fixed),
                  pl.BlockSpec((1, D_MODEL), fixed)],
        out_specs=[pl.BlockSpec((tm, D_MODEL), row)] * 2,
        out_shape=[jax.ShapeDtypeStruct((m, D_MODEL), F32), jax.ShapeDtypeStruct((m, D_MODEL), BF16)],
        scratch_shapes=[pltpu.VMEM((tm, D_MODEL), F32)],
        compiler_params=_cparams(2),
        name=name,
    )(h, wd, x32, g, b)


def _online_softmax_step(s, v, m_sc, l_sc, acc_sc):
    m_prev = m_sc[...]
    m_new = jnp.maximum(m_prev, jnp.max(s, axis=1, keepdims=True))
    alpha = jnp.exp(m_prev - m_new)
    p = jnp.exp(s - jnp.tile(m_new, (1, s.shape[1] // LANES)))
    l_sc[...] = alpha * l_sc[...] + jnp.sum(p, axis=1, keepdims=True)
    acc_sc[...] = alpha * acc_sc[...] + jnp.dot(p.astype(BF16), v, preferred_element_type=F32)
    m_sc[...] = m_new


def _softmax_scratch(tq):
    return [pltpu.VMEM((tq, LANES), F32), pltpu.VMEM((tq, LANES), F32), pltpu.VMEM((tq, HEAD_DIM), F32)]


def _init_softmax_state(m_sc, l_sc, acc_sc):
    m_sc[...] = jnp.full(m_sc.shape, NEG_INF, F32)
    l_sc[...] = jnp.zeros_like(l_sc)
    acc_sc[...] = jnp.zeros_like(acc_sc)


def _qk(q, k):
    return lax.dot_general(q, k, (((1,), (1,)), ((), ())), preferred_element_type=F32)


def _lane_column(x, idx):
    lane = lax.broadcasted_iota(jnp.int32, x.shape, 1)
    return jnp.sum(jnp.where(lane == idx, x, 0.0), axis=1, keepdims=True)


def _fox_prompt_body(q_ref, k_ref, v_ref, frow_ref, fcol_ref, o_ref, m_sc, l_sc, acc_sc, *, tq):
    h = pl.program_id(1)
    qi = pl.program_id(2)
    scale = HEAD_DIM ** -0.5
    q = q_ref[...]
    frow = jnp.broadcast_to(_lane_column(frow_ref[...], h), (tq, LANES))
    _init_softmax_state(m_sc, l_sc, acc_sc)

    def step(c, diagonal):
        start = pl.multiple_of(c * tq, tq)
        s = _qk(q, k_ref[pl.ds(start, tq), :]) * scale
        s = s + (jnp.tile(frow, (1, tq // LANES)) - fcol_ref[0, 0, :, pl.ds(start, tq)])
        if diagonal:
            row = lax.broadcasted_iota(jnp.int32, (tq, tq), 0)
            col = lax.broadcasted_iota(jnp.int32, (tq, tq), 1)
            s = jnp.where(col <= row, s, NEG_INF)
        _online_softmax_step(s, v_ref[pl.ds(start, tq), :], m_sc, l_sc, acc_sc)

    step(qi, True)

    def past(c, carry):
        step(c, False)
        return carry

    lax.fori_loop(0, qi, past, 0)
    o_ref[...] = (acc_sc[...] / l_sc[...]).astype(o_ref.dtype)


def _fox_prompt(q_maps, kb, vb, f_rows, f_cols, n_seq, name):
    m = q_maps.shape[0]
    t_len = m // n_seq
    tq = min(ATT_TILE, t_len)
    nq = t_len // tq
    return pl.pallas_call(
        functools.partial(_fox_prompt_body, tq=tq),
        grid=(n_seq, H_FOX, nq),
        in_specs=[pl.BlockSpec((tq, HEAD_DIM), lambda b, h, i: (b * nq + i, h)),
                  pl.BlockSpec((t_len, HEAD_DIM), lambda b, h, i: (b, h)),
                  pl.BlockSpec((t_len, HEAD_DIM), lambda b, h, i: (b, h)),
                  pl.BlockSpec((tq, LANES), lambda b, h, i: (b * nq + i, 0)),
                  pl.BlockSpec((1, 1, 1, t_len), lambda b, h, i: (b, h, 0, 0))],
        out_specs=pl.BlockSpec((tq, HEAD_DIM), lambda b, h, i: (b * nq + i, h)),
        out_shape=jax.ShapeDtypeStruct((m, H_FOX * HEAD_DIM), BF16),
        scratch_shapes=_softmax_scratch(tq),
        compiler_params=_cparams(3),
        name=name,
    )(q_maps, kb, vb, f_rows, f_cols)


def _bias_tiles_body(rel_ref, o_ref, *, tq):
    h = pl.program_id(0)
    row = lax.broadcasted_iota(jnp.int32, (tq, tq), 0)
    col = lax.broadcasted_iota(jnp.int32, (tq, tq), 1)
    for kind in range(2):
        d = row - col + kind * tq
        bias = jnp.full((tq, tq), rel_ref[h, 0], F32)
        for bkt in range(1, N_BUCKETS):
            bias = jnp.where(d >= T5_THRESHOLDS[bkt], rel_ref[h, bkt], bias)
        if kind == 0:
            bias = jnp.where(d >= 0, bias, NEG_INF)
        o_ref[0, kind] = bias


def _bias_tiles(rel_t, tq):
    nh = rel_t.shape[0]
    return pl.pallas_call(
        functools.partial(_bias_tiles_body, tq=tq),
        grid=(nh,),
        in_specs=[pl.BlockSpec(memory_space=pltpu.SMEM)],
        out_specs=pl.BlockSpec((1, 2, tq, tq), lambda h: (h, 0, 0, 0)),
        out_shape=jax.ShapeDtypeStruct((nh, 2, tq, tq), F32),
        compiler_params=_cparams(1),
        name="t5_bias_tiles",
    )(rel_t)


def _block_means_body(k_ref, o_ref, *, n_blocks):
    k = k_ref[...]
    o_ref[...] = jnp.zeros_like(o_ref)
    km = jnp.mean(k.reshape(n_blocks, MOBA_BLOCK, HEAD_DIM), axis=1)
    o_ref[0, 0, 0:n_blocks, :] = km


def _block_means(k32, n_seq, name):
    m = k32.shape[0]
    t_len = m // n_seq
    n_blocks = t_len // MOBA_BLOCK
    return pl.pallas_call(
        functools.partial(_block_means_body, n_blocks=n_blocks),
        grid=(n_seq, H_MOBA),
        in_specs=[pl.BlockSpec((t_len, HEAD_DIM), lambda b, h: (b, H_FOX + h))],
        out_specs=pl.BlockSpec((1, 1, LANES, HEAD_DIM), lambda b, h: (b, h, 0, 0)),
        out_shape=jax.ShapeDtypeStruct((n_seq, H_MOBA, LANES, HEAD_DIM), F32),
        compiler_params=_cparams(2),
        name=name,
    )(k32)


def _top_blocks(gate, n_valid, n_candidates, n_sel):
    lane = lax.broadcasted_iota(jnp.int32, gate.shape, 1)
    rank = jnp.zeros(gate.shape, F32)
    for c2 in range(n_candidates):
        g2 = gate[:, c2:c2 + 1]
        beats = jnp.where(g2 > gate, 1.0, jnp.where(g2 == gate, jnp.where(lane > c2, 1.0, 0.0), 0.0))
        rank = rank + beats * jnp.where(c2 < n_valid, 1.0, 0.0)
    return jnp.where(rank < n_sel, jnp.where(lane < n_valid, 1.0, 0.0), 0.0)


def _moba_prompt_body(far_ref, q_ref, k_ref, v_ref, means_ref, bias_ref, o_ref, m_sc, l_sc, acc_sc, mask_sc,
                      *, tq, t_len, n_candidates, n_sel):
    h = pl.program_id(1)
    qi = pl.program_id(2)
    scale = HEAD_DIM ** -0.5
    q = q_ref[...]
    gate = lax.dot_general(q.astype(F32), means_ref[0, 0], (((1,), (1,)), ((), ())),
                           precision=HIGHEST, preferred_element_type=F32)
    row = lax.broadcasted_iota(jnp.int32, (tq, LANES), 0)
    lane = lax.broadcasted_iota(jnp.int32, (tq, LANES), 1)
    own = qi * (tq // MOBA_BLOCK) + lax.div(row, MOBA_BLOCK)
    sel = jnp.where(lane == own, 1.0, _top_blocks(gate, own, n_candidates, n_sel))
    for b in range(t_len // MOBA_BLOCK):
        slab = jnp.where(sel[:, b:b + 1] > 0.5, 0.0, NEG_INF)
        mask_sc[:, b * LANES:(b + 1) * LANES] = jnp.broadcast_to(slab, (tq, LANES))
    _init_softmax_state(m_sc, l_sc, acc_sc)
    blocks_per_chunk = tq // MOBA_BLOCK

    def step(c, bias):
        start = pl.multiple_of(c * tq, tq)
        slabs = mask_sc[:, pl.ds(pl.multiple_of(c * blocks_per_chunk * LANES, blocks_per_chunk * LANES),
                                 blocks_per_chunk * LANES)]
        mask = jnp.concatenate([jnp.tile(slabs[:, j * LANES:(j + 1) * LANES], (1, MOBA_BLOCK // LANES))
                                for j in range(blocks_per_chunk)], axis=1)
        s = _qk(q, k_ref[pl.ds(start, tq), :]) * scale + bias + mask
        _online_softmax_step(s, v_ref[pl.ds(start, tq), :], m_sc, l_sc, acc_sc)

    step(qi, bias_ref[0, 0])

    @pl.when(qi > 0)
    def _():
        step(qi - 1, bias_ref[0, 1])

    far = far_ref[h]

    def past(c, carry):
        step(c, far)
        return carry

    lax.fori_loop(0, jnp.maximum(qi - 1, 0), past, 0)
    o_ref[...] = (acc_sc[...] / l_sc[...]).astype(o_ref.dtype)


def _moba_prompt(q_maps, kb, vb, means, bias_tiles, far, n_seq, name):
    m = q_maps.shape[0]
    t_len = m // n_seq
    tq = min(ATT_TILE, t_len)
    nq = t_len // tq
    n_candidates = max((t_len - 1) // MOBA_BLOCK, 1)
    n_sel = min(MOBA_TOPK, n_candidates)
    grid_spec = pltpu.PrefetchScalarGridSpec(
        num_scalar_prefetch=1,
        grid=(n_seq, H_MOBA, nq),
        in_specs=[pl.BlockSpec((tq, HEAD_DIM), lambda b, h, i, far: (b * nq + i, H_FOX + h)),
                  pl.BlockSpec((t_len, HEAD_DIM), lambda b, h, i, far: (b, H_FOX + h)),
                  pl.BlockSpec((t_len, HEAD_DIM), lambda b, h, i, far: (b, H_FOX + h)),
                  pl.BlockSpec((1, 1, LANES, HEAD_DIM), lambda b, h, i, far: (b, h, 0, 0)),
                  pl.BlockSpec((1, 2, tq, tq), lambda b, h, i, far: (h, 0, 0, 0))],
        out_specs=pl.BlockSpec((tq, HEAD_DIM), lambda b, h, i, far: (b * nq + i, h)),
        scratch_shapes=_softmax_scratch(tq) + [pltpu.VMEM((tq, (t_len // MOBA_BLOCK) * LANES), F32)],
    )
    return pl.pallas_call(
        functools.partial(_moba_prompt_body, tq=tq, t_len=t_len, n_candidates=n_candidates, n_sel=n_sel),
        grid_spec=grid_spec,
        out_shape=jax.ShapeDtypeStruct((m, H_MOBA * HEAD_DIM), BF16),
        compiler_params=_cparams(3),
        name=name,
    )(far, q_maps, kb, vb, means, bias_tiles)


def _diff_lambda(lam_ref, lam_init):
    r = lam_ref[...]
    s1 = jnp.sum(r[0:1, :] * r[1:2, :], axis=1, keepdims=True)
    s2 = jnp.sum(r[2:3, :] * r[3:4, :], axis=1, keepdims=True)
    return jnp.exp(s1) - jnp.exp(s2) + lam_init


def _diff_finish(o0, o1, lam, g, lam_init):
    o = o0 - lam * o1
    return o * lax.rsqrt(jnp.mean(o * o, axis=1, keepdims=True) + LN_EPS) * g * (1.0 - lam_init)


def _diff_prompt_body(far_ref, q_ref, k_ref, v_ref, bias_ref, lam_ref, g_ref, o_ref,
                      m0, l0, a0, m1, l1, a1, *, tq, lam_init):
    h = pl.program_id(1)
    qi = pl.program_id(2)
    scale = DIFF_QK_DIM ** -0.5
    q0 = q_ref[:, 0:HEAD_DIM]
    q1 = q_ref[:, HEAD_DIM:2 * HEAD_DIM]
    _init_softmax_state(m0, l0, a0)
    _init_softmax_state(m1, l1, a1)

    def step(c, bias):
        start = pl.multiple_of(c * tq, tq)
        k = k_ref[pl.ds(start, tq), :]
        v = v_ref[pl.ds(start, tq), :]
        _online_softmax_step(_qk(q0, k) * scale + bias, v, m0, l0, a0)
        _online_softmax_step(_qk(q1, k) * scale + bias, v, m1, l1, a1)

    step(qi, bias_ref[0, 0])

    @pl.when(qi > 0)
    def _():
        step(qi - 1, bias_ref[0, 1])

    far = far_ref[H_MOBA + h]

    def past(c, carry):
        step(c, far)
        return carry

    lax.fori_loop(0, jnp.maximum(qi - 1, 0), past, 0)
    lam = _diff_lambda(lam_ref, lam_init)
    o = _diff_finish(a0[...] / l0[...], a1[...] / l1[...], lam, g_ref[...], lam_init)
    o_ref[...] = o.astype(o_ref.dtype)


def _diff_prompt(q_maps, kb, vb, bias_tiles, far, lam_rows, g, lam_init, n_seq, name):
    m = q_maps.shape[0]
    t_len = m // n_seq
    tq = min(ATT_TILE, t_len)
    nq = t_len // tq
    h0 = H_FOX + H_MOBA
    grid_spec = pltpu.PrefetchScalarGridSpec(
        num_scalar_prefetch=1,
        grid=(n_seq, H_DIFF, nq),
        in_specs=[pl.BlockSpec((tq, 2 * HEAD_DIM), lambda b, h, i, far: (b * nq + i, h0 // 2 + h)),
                  pl.BlockSpec((t_len, HEAD_DIM), lambda b, h, i, far: (b, h0 + h)),
                  pl.BlockSpec((t_len, HEAD_DIM), lambda b, h, i, far: (b, h0 + h)),
                  pl.BlockSpec((1, 2, tq, tq), lambda b, h, i, far: (H_MOBA + h, 0, 0, 0)),
                  pl.BlockSpec((8, LANES), lambda b, h, i, far: (0, 0)),
                  pl.BlockSpec((1, HEAD_DIM), lambda b, h, i, far: (0, 0))],
        out_specs=pl.BlockSpec((tq, HEAD_DIM), lambda b, h, i, far: (b * nq + i, h)),
        scratch_shapes=_softmax_scratch(tq) + _softmax_scratch(tq),
    )
    return pl.pallas_call(
        functools.partial(_diff_prompt_body, tq=tq, lam_init=lam_init),
        grid_spec=grid_spec,
        out_shape=jax.ShapeDtypeStruct((m, H_DIFF * HEAD_DIM), BF16),
        compiler_params=_cparams(3),
        name=name,
    )(far, q_maps, kb, vb, bias_tiles, lam_rows, g)


def _sample_page_body(pt_ref, q_ref, *refs):
    g = SAMPLE_PAGES_PER_STEP
    k_refs, v_refs, sfx_refs = refs[0:g], refs[g:2 * g], refs[2 * g:3 * g]
    cb_ref, m_ref, l_ref, acc_ref, ksum_ref = refs[3 * g:]
    n_cols = PAGE_SIZE * N_HEADS
    row = lax.broadcasted_iota(jnp.int32, (SAMPLE_ROWS, n_cols), 0)
    col = lax.broadcasted_iota(jnp.int32, (SAMPLE_ROWS, n_cols), 1)
    scale = jnp.where(row < H_FOX + H_MOBA, HEAD_DIM ** -0.5, DIFF_QK_DIM ** -0.5).astype(F32)
    same_head = (col & (N_HEADS - 1)) == (row & (N_HEADS - 1))
    q = q_ref[0]
    for j in range(g):
        kp = k_refs[j][0, 0]
        kf = kp.reshape(n_cols, HEAD_DIM).astype(BF16)
        vf = v_refs[j][0, 0].reshape(n_cols, HEAD_DIM).astype(BF16)
        s = _qk(q, kf) * scale + (sfx_refs[j][0, 0] + cb_ref[0, j])
        s = jnp.where(same_head, s, NEG_INF)
        m = jnp.max(s, axis=1, keepdims=True)
        p = jnp.exp(s - m)
        l = jnp.sum(p, axis=1, keepdims=True)
        acc_ref[0, j] = jnp.dot(p.astype(BF16), vf, preferred_element_type=F32)
        m_ref[0, j] = jnp.broadcast_to(m, (SAMPLE_ROWS, LANES))
        l_ref[0, j] = jnp.broadcast_to(l, (SAMPLE_ROWS, LANES))
        ksum_ref[0, j] = jnp.sum(kp, axis=0)


def _sample_pages(layer, page_table, q_rows, cache_k, cache_v, sfx_rows, const_rows, name):
    n_seq, n_pages = page_table.shape
    g = SAMPLE_PAGES_PER_STEP
    n_cols = PAGE_SIZE * N_HEADS
    page_block = (1, 1, PAGE_SIZE, N_HEADS, HEAD_DIM)

    def page_spec(j):
        return pl.BlockSpec(page_block, lambda b, p, pt: (layer, pt[b, p * g + j], 0, 0, 0))

    def sfx_spec(j):
        return pl.BlockSpec((1, 1, 1, n_cols), lambda b, p, pt: (0, pt[b, p * g + j], 0, 0))

    stat_spec = pl.BlockSpec((1, g, SAMPLE_ROWS, LANES), lambda b, p, pt: (b, p, 0, 0))
    grid_spec = pltpu.PrefetchScalarGridSpec(
        num_scalar_prefetch=1,
        grid=(n_seq, n_pages // g),
        in_specs=([pl.BlockSpec((1, SAMPLE_ROWS, HEAD_DIM), lambda b, p, pt: (b, 0, 0))]
                  + [page_spec(j) for j in range(g)] + [page_spec(j) for j in range(g)]
                  + [sfx_spec(j) for j in range(g)]
                  + [pl.BlockSpec((1, g, 1, n_cols), lambda b, p, pt: (b, p, 0, 0))]),
        out_specs=[stat_spec, stat_spec, stat_spec,
                   pl.BlockSpec((1, g, N_HEADS, HEAD_DIM), lambda b, p, pt: (b, p, 0, 0))],
    )
    stat_shape = jax.ShapeDtypeStruct((n_seq, n_pages, SAMPLE_ROWS, LANES), F32)
    return pl.pallas_call(
        _sample_page_body,
        grid_spec=grid_spec,
        out_shape=[stat_shape, stat_shape, stat_shape,
                   jax.ShapeDtypeStruct((n_seq, n_pages, N_HEADS, HEAD_DIM), F32)],
        compiler_params=_cparams(2),
        name=name,
    )(page_table, q_rows, *([cache_k] * g), *([cache_v] * g), *([sfx_rows] * g), const_rows)


def _sample_combine_body(m_ref, l_ref, acc_ref, ksum_ref, q_ref, k_ref, v_ref, sb_ref, lam_ref, g_ref, o_ref,
                         *, n_pages, lam_init):
    n_blocks = n_pages * PAGE_SIZE // MOBA_BLOCK
    pages_per_block = MOBA_BLOCK // PAGE_SIZE
    n_sel = min(MOBA_TOPK, n_blocks)
    m = m_ref[0]
    l = l_ref[0]
    acc = acc_ref[0]
    q = q_ref[0]
    head = lax.broadcasted_iota(jnp.int32, (N_HEADS, LANES), 0)
    is_moba = jnp.where(head >= H_FOX, jnp.where(head < H_FOX + H_MOBA, 1.0, 0.0), 0.0)

    gp = jnp.sum(ksum_ref[0] * q[0:N_HEADS][None], axis=2, keepdims=True)
    gb = jnp.sum(gp.reshape(n_blocks, pages_per_block, N_HEADS, 1), axis=1) * (1.0 / MOBA_BLOCK)
    gb = jnp.broadcast_to(gb, (n_blocks, N_HEADS, LANES))
    blk = lax.broadcasted_iota(jnp.int32, (n_blocks, N_HEADS, LANES), 0)
    rank = jnp.zeros((n_blocks, N_HEADS, LANES), F32)
    for c2 in range(n_blocks):
        g2 = gb[c2:c2 + 1]
        rank = rank + jnp.where(g2 > gb, 1.0, jnp.where(g2 == gb, jnp.where(blk > c2, 1.0, 0.0), 0.0))
    sel_b = jnp.where(rank < n_sel, 1.0, 0.0)
    sel_b = jnp.maximum(sel_b, 1.0 - is_moba[None])
    sel_p = jnp.broadcast_to(sel_b[:, None], (n_blocks, pages_per_block, N_HEADS, LANES))
    sel_p = sel_p.reshape(n_pages, N_HEADS, LANES)
    sel = jnp.concatenate([sel_p, jnp.ones_like(sel_p)], axis=1)

    row = lax.broadcasted_iota(jnp.int32, (SAMPLE_ROWS, 1), 0)
    scale = jnp.where(row < H_FOX + H_MOBA, HEAD_DIM ** -0.5, DIFF_QK_DIM ** -0.5).astype(F32)
    s_self = jnp.sum(q * k_ref[0], axis=1, keepdims=True) * scale + sb_ref[0]

    m_sel = jnp.where(sel > 0.0, m, NEG_INF)
    m_all = jnp.maximum(jnp.max(m_sel, axis=0), s_self)
    w = jnp.where(sel > 0.0, jnp.exp(m_sel - m_all[None]), 0.0)
    w_self = jnp.exp(s_self - m_all)
    den = jnp.sum(w * l, axis=0) + w_self
    num = jnp.sum(w * acc, axis=0) + w_self * v_ref[0]
    o = num / den
    o_first, o_second = o[0:N_HEADS], o[N_HEADS:SAMPLE_ROWS]
    lam = _diff_lambda(lam_ref, lam_init)
    od = _diff_finish(o_first, o_second, lam, g_ref[...], lam_init)
    o_ref[0] = jnp.where(head >= H_FOX + H_MOBA, od, o_first).astype(o_ref.dtype)


def _sample_combine(m, l, acc, ksum, q_rows32, k_rows, v_rows, self_bias, lam_rows, g, lam_init, name):
    n_seq, n_pages = m.shape[:2]
    stat_spec = pl.BlockSpec((1, n_pages, SAMPLE_ROWS, LANES), lambda b: (b, 0, 0, 0))
    row_spec = pl.BlockSpec((1, SAMPLE_ROWS, HEAD_DIM), lambda b: (b, 0, 0))
    return pl.pallas_call(
        functools.partial(_sample_combine_body, n_pages=n_pages, lam_init=lam_init),
        grid=(n_seq,),
        in_specs=[stat_spec, stat_spec, stat_spec,
                  pl.BlockSpec((1, n_pages, N_HEADS, HEAD_DIM), lambda b: (b, 0, 0, 0)),
                  row_spec, row_spec, row_spec,
                  pl.BlockSpec((1, SAMPLE_ROWS, LANES), lambda b: (0, 0, 0)),
                  pl.BlockSpec((8, LANES), lambda b: (0, 0)),
                  pl.BlockSpec((1, HEAD_DIM), lambda b: (0, 0))],
        out_specs=pl.BlockSpec((1, N_HEADS, HEAD_DIM), lambda b: (b, 0, 0)),
        out_shape=jax.ShapeDtypeStruct((n_seq, N_HEADS, HEAD_DIM), BF16),
        compiler_params=_cparams(1),
        name=name,
    )(m, l, acc, ksum, q_rows32, k_rows, v_rows, self_bias, lam_rows, g)


def _router_body(x_ref, w_ref, b_ref, o_ref):
    logits = jnp.dot(x_ref[...], w_ref[...], precision=HIGHEST, preferred_element_type=F32) + b_ref[...]
    lane = lax.broadcasted_iota(jnp.int32, logits.shape, 1)
    lane_f = lane.astype(F32)
    logits = jnp.where(lane < N_EXPERTS, logits, NEG_INF)
    v1 = jnp.max(logits, axis=1, keepdims=True)
    i1 = jnp.min(jnp.where(logits == v1, lane_f, float(LANES)), axis=1, keepdims=True)
    rest = jnp.where(lane_f == i1, NEG_INF, logits)
    v2 = jnp.max(rest, axis=1, keepdims=True)
    i2 = jnp.min(jnp.where(rest == v2, lane_f, float(LANES)), axis=1, keepdims=True)
    e2 = jnp.exp(v2 - v1)
    g1 = 1.0 / (1.0 + e2)
    g2 = e2 / (1.0 + e2)
    out = jnp.where(lane == 0, i1, jnp.where(lane == 1, i2, jnp.where(lane == 2, g1, jnp.where(lane == 3, g2, 0.0))))
    o_ref[...] = out


def _router(x32, w_r, b_r, tm, name):
    m = x32.shape[0]
    tm = min(tm, m)
    return pl.pallas_call(
        _router_body,
        grid=(m // tm,),
        in_specs=[pl.BlockSpec((tm, D_MODEL), lambda i: (i, 0)),
                  pl.BlockSpec((D_MODEL, LANES), lambda i: (0, 0)),
                  pl.BlockSpec((1, LANES), lambda i: (0, 0))],
        out_specs=pl.BlockSpec((tm, LANES), lambda i: (i, 0)),
        out_shape=jax.ShapeDtypeStruct((m, LANES), F32),
        compiler_params=_cparams(1),
        name=name,
    )(x32, w_r, b_r)


def _slab_copy(src_hbm, idx, dst, r, sem):
    return pltpu.make_async_copy(src_hbm.at[pl.ds(pl.multiple_of(idx * SLAB, SLAB), SLAB), :],
                                 dst.at[pl.ds(pl.multiple_of(r * SLAB, SLAB), SLAB), :], sem)


def _gather_slabs(src_hbm, idx_ref, dst, sem, n_rows):
    def start(r0, carry):
        for u in range(GATHER_UNROLL):
            r = r0 * GATHER_UNROLL + u
            _slab_copy(src_hbm, idx_ref[0, 0, r], dst, r, sem).start(priority=u % 2)
        return carry

    lax.fori_loop(0, n_rows // GATHER_UNROLL, start, 0)
    pltpu.make_async_copy(src_hbm.at[pl.ds(0, n_rows * SLAB), :], dst, sem).wait()


def _slab_columns(buf, c, n_rows):
    return buf[pl.ds(c, n_rows, stride=SLAB), :]


def _store_as_slabs(ref, y):
    n_rows = y.shape[0]
    for c in range(SLAB):
        ref[pl.ds(c, n_rows, stride=SLAB), :] = y[:, c * LANES:(c + 1) * LANES]


def _dispatch_body(idx_ref, x_hbm, o_ref, buf, sem, *, tr):
    _gather_slabs(x_hbm, idx_ref, buf, sem, tr)
    for c in range(SLAB):
        o_ref[:, c * LANES:(c + 1) * LANES] = _slab_columns(buf, c, tr).astype(o_ref.dtype)


def _dispatch(x_slabs, row_token, tr, name):
    p = row_token.shape[0]
    idx = row_token.reshape(p // tr, 1, tr)
    return pl.pallas_call(
        functools.partial(_dispatch_body, tr=tr),
        grid=(p // tr,),
        in_specs=[pl.BlockSpec((1, 1, tr), lambda i: (i, 0, 0), memory_space=pltpu.SMEM),
                  pl.BlockSpec(memory_space=pl.ANY)],
        out_specs=pl.BlockSpec((tr, D_MODEL), lambda i: (i, 0)),
        out_shape=jax.ShapeDtypeStruct((p, D_MODEL), BF16),
        scratch_shapes=[pltpu.VMEM((tr * SLAB, LANES), F32), pltpu.SemaphoreType.DMA(())],
        compiler_params=_cparams(1),
        name=name,
    )(idx, x_slabs)


def _expert_up_body(te_ref, nu_ref, x_ref, wg_ref, wu_ref, h_ref):
    @pl.when(pl.program_id(0) < nu_ref[0])
    def _():
        x = x_ref[...]
        g = jnp.dot(x, wg_ref[0], preferred_element_type=F32)
        u = jnp.dot(x, wu_ref[0], preferred_element_type=F32)
        h_ref[...] = (g * (1.0 / (1.0 + jnp.exp(-g))) * u).astype(h_ref.dtype)

    @pl.when(pl.program_id(0) >= nu_ref[0])
    def _():
        h_ref[...] = jnp.zeros_like(h_ref)


def _expert_up(xs, wg, wu, tile_expert, n_used, tm, tn, name):
    p, k = xs.shape
    f = wg.shape[2]
    tn = min(tn, f)
    nf = f // tn
    wspec = pl.BlockSpec((1, k, tn), lambda i, j, te, nu: (te[i], 0, jnp.where(i < nu[0], j, nf - 1)))
    grid_spec = pltpu.PrefetchScalarGridSpec(
        num_scalar_prefetch=2,
        grid=(p // tm, nf),
        in_specs=[pl.BlockSpec((tm, k), lambda i, j, te, nu: (i, 0)), wspec, wspec],
        out_specs=pl.BlockSpec((tm, tn), lambda i, j, te, nu: (i, j)),
    )
    return pl.pallas_call(
        _expert_up_body,
        grid_spec=grid_spec,
        out_shape=jax.ShapeDtypeStruct((p, f), BF16),
        compiler_params=_cparams(2),
        name=name,
    )(tile_expert, n_used, xs, wg, wu)


def _expert_down_body(te_ref, nu_ref, h_ref, w_ref, y_ref, acc):
    kk = pl.program_id(1)

    @pl.when(pl.program_id(0) < nu_ref[0])
    def _():
        @pl.when(kk == 0)
        def _():
            acc[...] = jnp.zeros_like(acc)

        acc[...] += jnp.dot(h_ref[...], w_ref[0], preferred_element_type=F32)

        @pl.when(kk == pl.num_programs(1) - 1)
        def _():
            _store_as_slabs(y_ref, acc[...])

    @pl.when(pl.program_id(0) >= nu_ref[0])
    def _():
        y_ref[...] = jnp.zeros_like(y_ref)


def _expert_down(h, wd, tile_expert, n_used, tm, tk, name):
    p, f = h.shape
    tk = min(tk, f)
    nk = f // tk
    grid_spec = pltpu.PrefetchScalarGridSpec(
        num_scalar_prefetch=2,
        grid=(p // tm, nk),
        in_specs=[pl.BlockSpec((tm, tk), lambda i, k, te, nu: (i, k)),
                  pl.BlockSpec((1, tk, D_MODEL), lambda i, k, te, nu: (te[i], jnp.where(i < nu[0], k, nk - 1), 0))],
        out_specs=pl.BlockSpec((tm * SLAB, LANES), lambda i, k, te, nu: (i, 0)),
        scratch_shapes=[pltpu.VMEM((tm, D_MODEL), F32)],
    )
    return pl.pallas_call(
        _expert_down_body,
        grid_spec=grid_spec,
        out_shape=jax.ShapeDtypeStruct((p * SLAB, LANES), F32),
        compiler_params=_cparams(2),
        name=name,
    )(tile_expert, n_used, h, wd)


def _moe_combine_body(p0_ref, p1_ref, y_hbm, r_ref, x_ref, g_ref, b_ref, o32_ref, ob_ref, buf0, buf1, mix, sem0, sem1,
                      *, tr):
    _gather_slabs(y_hbm, p0_ref, buf0, sem0, tr)
    _gather_slabs(y_hbm, p1_ref, buf1, sem1, tr)
    r = r_ref[...]
    gate0 = jnp.broadcast_to(r[:, 2:3], (tr, LANES))
    gate1 = jnp.broadcast_to(r[:, 3:4], (tr, LANES))
    for c in range(SLAB):
        mix[:, c * LANES:(c + 1) * LANES] = gate0 * _slab_columns(buf0, c, tr) + gate1 * _slab_columns(buf1, c, tr)
    y = _layer_norm_rows(ALPHA * x_ref[...] + mix[...], g_ref[...], b_ref[...])
    o32_ref[...] = y
    ob_ref[...] = y.astype(BF16)


def _moe_combine(y_sorted, pos0, pos1, routes, x32, g, b, tr, name):
    m = x32.shape[0]
    tr = min(tr, m)
    idx_spec = pl.BlockSpec((1, 1, tr), lambda i: (i, 0, 0), memory_space=pltpu.SMEM)
    row = lambda i: (i, 0)
    fixed = lambda i: (0, 0)
    return pl.pallas_call(
        functools.partial(_moe_combine_body, tr=tr),
        grid=(m // tr,),
        in_specs=[idx_spec, idx_spec,
                  pl.BlockSpec(memory_space=pl.ANY),
                  pl.BlockSpec((tr, LANES), row),
                  pl.BlockSpec((tr, D_MODEL), row),
                  pl.BlockSpec((1, D_MODEL), fixed),
                  pl.BlockSpec((1, D_MODEL), fixed)],
        out_specs=[pl.BlockSpec((tr, D_MODEL), row)] * 2,
        out_shape=[jax.ShapeDtypeStruct((m, D_MODEL), F32), jax.ShapeDtypeStruct((m, D_MODEL), BF16)],
        scratch_shapes=[pltpu.VMEM((tr * SLAB, LANES), F32), pltpu.VMEM((tr * SLAB, LANES), F32),
                        pltpu.VMEM((tr, D_MODEL), F32),
                        pltpu.SemaphoreType.DMA(()), pltpu.SemaphoreType.DMA(())],
        compiler_params=_cparams(1),
        name=name,
    )(pos0.reshape(m // tr, 1, tr), pos1.reshape(m // tr, 1, tr), y_sorted, routes, x32, g, b)


def _routing_plan(routes, tm):
    m = routes.shape[0]
    experts = jnp.concatenate([routes[:, 0], routes[:, 1]]).astype(jnp.int32)
    onehot = (experts[:, None] == jnp.arange(N_EXPERTS, dtype=jnp.int32)[None, :]).astype(jnp.int32)
    csum = jnp.cumsum(onehot, axis=0)
    rank = jnp.sum(csum * onehot, axis=1) - 1
    counts = csum[-1]
    padded = ((counts + tm - 1) // tm) * tm
    ends = jnp.cumsum(padded)
    starts = ends - padded
    pos = jnp.sum(onehot * starts[None, :], axis=1) + rank
    n_rows = (pl.cdiv(TOP_K * m, tm) + N_EXPERTS) * tm
    token = jnp.concatenate([jnp.arange(m, dtype=jnp.int32)] * TOP_K)
    row_token = jnp.zeros((n_rows,), jnp.int32).at[pos].set(token)
    tile_start = jnp.arange(n_rows // tm, dtype=jnp.int32) * tm
    tile_expert = jnp.minimum(jnp.sum((tile_start[:, None] >= ends[None, :]).astype(jnp.int32), axis=1),
                              N_EXPERTS - 1)
    n_used = (ends[-1] // tm).reshape(1).astype(jnp.int32)
    return row_token, pos[:m], pos[m:], tile_expert, n_used


def _moe_layer(x32, x_slabs, w_r, b_r, wg, wu, wd, g, b, tm, tn, tk, tag):
    routes = _router(x32, w_r, b_r, 512, f"router_{tag}")
    row_token, pos0, pos1, tile_expert, n_used = _routing_plan(routes, tm)
    xs = _dispatch(x_slabs, row_token, tm, f"moe_dispatch_{tag}")
    h = _expert_up(xs, wg, wu, tile_expert, n_used, tm, tn, f"moe_up_{tag}")
    y = _expert_down(h, wd, tile_expert, n_used, tm, tk, f"moe_down_{tag}")
    return _moe_combine(y, pos0, pos1, routes, x32, g, b, 256, f"moe_combine_{tag}")


def _q_map_weights(w_q):
    h0 = (H_FOX + H_MOBA) * HEAD_DIM
    cols = [w_q[:, :h0]]
    zeros = jnp.zeros((w_q.shape[0], DIFF_QK_DIM), w_q.dtype)
    for i in range(H_DIFF):
        wh = w_q[:, h0 + i * HEAD_DIM:h0 + (i + 1) * HEAD_DIM]
        cols += [wh[:, :DIFF_QK_DIM], zeros, zeros, wh[:, DIFF_QK_DIM:]]
    return jnp.concatenate(cols, axis=1)


def _sample_row_order():
    first = list(range(H_FOX + H_MOBA)) + [H_FOX + H_MOBA + 2 * i for i in range(H_DIFF)]
    second = [H_FOX + H_MOBA + 2 * i + 1 for i in range(H_DIFF)]
    return first, second


def kernel(x_prompt, x_sample, cache_k, cache_v, cache_logf, page_table, w_in, b_forget, w_out, lam_q1, lam_k1, lam_q2, lam_k2, subln_g, rel_bias, ln1_g, ln1_b, ln2_g, ln2_b, w_ffn_gate, w_ffn_up, w_ffn_down, w_router, b_router, w_exp_gate, w_exp_up, w_exp_down):
    n_seq, t_len, _ = x_prompt.shape
    n_dec, dec_seq, _ = x_sample.shape
    assert dec_seq == 1 and t_len % MOBA_BLOCK == 0
    n_pages = page_table.shape[1]
    past = n_pages * PAGE_SIZE
    assert past % MOBA_BLOCK == 0 and past >= MOBA_BLOCK and n_pages % SAMPLE_PAGES_PER_STEP == 0
    depth = w_in.shape[0]
    n_pool = cache_k.shape[1]
    mp, ms = n_seq * t_len, n_dec
    n_cols = PAGE_SIZE * N_HEADS

    rel_t = jnp.transpose(rel_bias).astype(F32)
    far = rel_t[:, N_BUCKETS - 1]
    bias_tiles = _bias_tiles(rel_t, min(ATT_TILE, t_len))
    dist = past - (jnp.arange(n_pages)[:, None] * PAGE_SIZE + jnp.arange(PAGE_SIZE)[None, :])
    bucket_of = jnp.array([_t5_bucket_static(d) for d in range(T5_FAR + 1)], jnp.int32)
    bkt = bucket_of[jnp.minimum(dist, T5_FAR)]
    t5_cols = jnp.concatenate([jnp.zeros((n_pages, PAGE_SIZE, H_FOX), F32), rel_bias[bkt]], axis=2)
    first_rows, second_rows = _sample_row_order()
    self_bias16 = jnp.concatenate([jnp.zeros((H_FOX,), F32), rel_bias[0]])
    self_bias = jnp.broadcast_to(jnp.concatenate([self_bias16, self_bias16])[None, :, None],
                                 (1, SAMPLE_ROWS, LANES))

    xp32 = x_prompt.reshape(mp, D_MODEL)
    xs32 = x_sample.reshape(ms, D_MODEL)
    xpb, xsb = xp32.astype(BF16), xs32.astype(BF16)
    outs = {k: [] for k in ("kp", "vp", "fp", "ks", "vs", "fs")}

    for l in range(depth):
        lam_init = 0.8 - 0.6 * math.exp(-0.3 * l)
        w_q = _q_map_weights(w_in[l, :, :D_MODEL]).astype(BF16)
        w_k = w_in[l, :, D_MODEL:2 * D_MODEL].astype(BF16)
        w_v = w_in[l, :, 2 * D_MODEL:3 * D_MODEL].astype(BF16)
        w_f = jnp.pad(w_in[l, :, 3 * D_MODEL:], ((0, 0), (0, LANES - H_FOX)))
        b_f = jnp.pad(b_forget[l], (0, LANES - H_FOX)).reshape(1, LANES)
        w_o = w_out[l].astype(BF16)
        lam_rows = jnp.pad(jnp.stack([lam_q1[l], lam_k1[l], lam_q2[l], lam_k2[l]]),
                           ((0, 4), (0, LANES - DIFF_QK_DIM)))
        g_sub = subln_g[l].reshape(1, HEAD_DIM)
        g1, b1 = ln1_g[l].reshape(1, D_MODEL), ln1_b[l].reshape(1, D_MODEL)
        g2, b2 = ln2_g[l].reshape(1, D_MODEL), ln2_b[l].reshape(1, D_MODEL)

        (qp,) = _matmul(xpb, w_q, [BF16], 1024, 1280, f"proj_q_p{l}")
        kp32, kpb = _matmul(xpb, w_k, [F32, BF16], 1024, 1024, f"proj_k_p{l}")
        vp32, vpb = _matmul(xpb, w_v, [F32, BF16], 1024, 1024, f"proj_v_p{l}")
        lfp, fcum = _log_forget(xp32, w_f, b_f, n_seq, True, f"logf_p{l}")
        f_cols = jnp.transpose(fcum.reshape(n_seq, t_len, LANES)[:, :, :8], (0, 2, 1)).reshape(n_seq, 8, 1, t_len)
        o_fox = _fox_prompt(qp, kpb, vpb, fcum, f_cols, n_seq, f"fox_p{l}")
        means = _block_means(kp32, n_seq, f"moba_means_p{l}")
        o_moba = _moba_prompt(qp, kpb, vpb, means, bias_tiles, far, n_seq, f"moba_p{l}")
        o_diff = _diff_prompt(qp, kpb, vpb, bias_tiles, far, lam_rows, g_sub, lam_init, n_seq, f"diff_p{l}")
        is_moe = l % 2 == 1
        xp32, xpb, *xp_slabs = _outproj_ln(o_fox, o_moba, o_diff, w_o, xp32, g1, b1, 256, is_moe, f"outproj_p{l}")

        (qs,) = _matmul(xsb, w_q, [F32], 512, 512, f"proj_q_s{l}")
        ks32, _ = _matmul(xsb, w_k, [F32, BF16], 512, 512, f"proj_k_s{l}")
        vs32, _ = _matmul(xsb, w_v, [F32, BF16], 512, 512, f"proj_v_s{l}")
        lfs, _ = _log_forget(xs32, w_f, b_f, n_dec, False, f"logf_s{l}")
        qs = qs.reshape(ms, N_QMAPS, HEAD_DIM)
        pad_rows = jnp.zeros((ms, SAMPLE_ROWS - N_HEADS - H_DIFF, HEAD_DIM), F32)
        q_rows32 = jnp.concatenate([qs[:, jnp.array(first_rows)], pad_rows, qs[:, jnp.array(second_rows)]], axis=1)
        k_heads = ks32.reshape(ms, N_HEADS, HEAD_DIM)
        v_heads = vs32.reshape(ms, N_HEADS, HEAD_DIM)
        k_rows = jnp.concatenate([k_heads, k_heads], axis=1)
        v_rows = jnp.concatenate([v_heads, v_heads], axis=1)
        lf_pool = cache_logf[l].astype(F32)
        pool_tot = jnp.sum(lf_pool, axis=1)
        sfx_pool = pool_tot[:, None, :] - jnp.cumsum(lf_pool, axis=1)
        sfx_rows = jnp.pad(sfx_pool, ((0, 0), (0, 0), (0, N_HEADS - H_FOX))).reshape(1, n_pool, 1, n_cols)
        page_tot = pool_tot[page_table]
        later = jnp.sum(page_tot, axis=1, keepdims=True) - jnp.cumsum(page_tot, axis=1)
        fox_const = later + lfs[:, None, :H_FOX]
        const16 = jnp.concatenate(
            [jnp.broadcast_to(fox_const[:, :, None, :], (ms, n_pages, PAGE_SIZE, H_FOX)),
             jnp.broadcast_to(t5_cols[None, :, :, H_FOX:], (ms, n_pages, PAGE_SIZE, N_HEADS - H_FOX))], axis=3)
        const_rows = const16.reshape(ms, n_pages, 1, n_cols)
        m_s, l_s, acc_s, ksum = _sample_pages(l, page_table, q_rows32.astype(BF16), cache_k, cache_v,
                                              sfx_rows, const_rows, f"sample_pages{l}")
        o_s = _sample_combine(m_s, l_s, acc_s, ksum, q_rows32, k_rows, v_rows, self_bias, lam_rows, g_sub,
                              lam_init, f"sample_combine{l}").reshape(ms, D_MODEL)
        nf, nm = H_FOX * HEAD_DIM, (H_FOX + H_MOBA) * HEAD_DIM
        xs32, xsb, *xs_slabs = _outproj_ln(o_s[:, :nf], o_s[:, nf:nm], o_s[:, nm:], w_o, xs32, g1, b1, 256, is_moe,
                                           f"outproj_s{l}")

        i = l // 2
        if l % 2 == 0:
            wg, wu, wd = (w_ffn_gate[i].astype(BF16), w_ffn_up[i].astype(BF16), w_ffn_down[i].astype(BF16))
            hp = _swiglu_up(xpb, wg, wu, 1024, 512, f"ffn_up_p{l}")
            xp32, xpb = _down_ln(hp, wd, xp32, g2, b2, 512, 1408, f"ffn_down_p{l}")
            hs = _swiglu_up(xsb, wg, wu, 1024, 512, f"ffn_up_s{l}")
            xs32, xsb = _down_ln(hs, wd, xs32, g2, b2, 512, 1408, f"ffn_down_s{l}")
        else:
            wg, wu, wd = (w_exp_gate[i].astype(BF16), w_exp_up[i].astype(BF16), w_exp_down[i].astype(BF16))
            w_r = jnp.pad(w_router[i], ((0, 0), (0, LANES - N_EXPERTS)))
            b_r = jnp.pad(b_router[i], (0, LANES - N_EXPERTS)).reshape(1, LANES)
            xp32, xpb = _moe_layer(xp32, xp_slabs[0], w_r, b_r, wg, wu, wd, g2, b2, 512, 512, 1408, f"p{l}")
            xs32, xsb = _moe_layer(xs32, xs_slabs[0], w_r, b_r, wg, wu, wd, g2, b2, 16, 512, 1408, f"s{l}")

        outs["kp"].append(kp32.reshape(n_seq, t_len // PAGE_SIZE, PAGE_SIZE, N_HEADS, HEAD_DIM))
        outs["vp"].append(vp32.reshape(n_seq, t_len // PAGE_SIZE, PAGE_SIZE, N_HEADS, HEAD_DIM))
        outs["fp"].append(lfp[:, :H_FOX].reshape(n_seq, t_len // PAGE_SIZE, PAGE_SIZE, H_FOX))
        outs["ks"].append(ks32.reshape(n_dec, 1, N_HEADS, HEAD_DIM))
        outs["vs"].append(vs32.reshape(n_dec, 1, N_HEADS, HEAD_DIM))
        outs["fs"].append(lfs[:, :H_FOX].reshape(n_dec, 1, H_FOX))

    return (xp32.reshape(n_seq, t_len, D_MODEL), xs32.reshape(n_dec, 1, D_MODEL),
            jnp.stack(outs["kp"]), jnp.stack(outs["vp"]), jnp.stack(outs["fp"]),
            jnp.stack(outs["ks"]), jnp.stack(outs["vs"]), jnp.stack(outs["fs"]))
```

```python
import functools
import math

import jax
import jax.numpy as jnp
from jax import lax
from jax.experimental import pallas as pl
from jax.experimental.pallas import tpu as pltpu

F32 = jnp.float32
BF16 = jnp.bfloat16
HIGHEST = lax.Precision.HIGHEST
NEG_INF = float("-inf")

HEAD_DIM = 128
N_HEADS = 16
D_MODEL = N_HEADS * HEAD_DIM
H_FOX = 6
H_MOBA = 6
H_DIFF = 4
DIFF_QK_DIM = HEAD_DIM // 2
N_QMAPS = H_FOX + H_MOBA + 2 * H_DIFF
MOBA_BLOCK = 256
MOBA_TOPK = 3
PAGE_SIZE = 128
N_BUCKETS = 32
MAX_DISTANCE = 128
N_EXPERTS = 8
TOP_K = 2
DEPTH = 2
ALPHA = (2.0 * DEPTH) ** 0.25
LN_EPS = 1e-5

LANES = 128
VMEM_LIMIT_BYTES = 56 * 1024 * 1024
ATT_TILE = 2 * MOBA_BLOCK
LOGF_TILE = 256
SAMPLE_ROWS = 2 * N_HEADS
SAMPLE_PAGES_PER_STEP = 4
GATHER_UNROLL = 8


def _cparams(n_axes):
    return pltpu.CompilerParams(
        dimension_semantics=("arbitrary",) * n_axes, vmem_limit_bytes=VMEM_LIMIT_BYTES)


def _t5_thresholds():
    max_exact = N_BUCKETS // 2

    def bucket(d):
        if d < max_exact:
            return d
        r = math.log(d / max_exact) / math.log(MAX_DISTANCE / max_exact)
        return min(max_exact + int(r * (N_BUCKETS - max_exact)), N_BUCKETS - 1)

    return [min(d for d in range(2 * MAX_DISTANCE) if bucket(d) >= b) for b in range(N_BUCKETS)]


T5_THRESHOLDS = _t5_thresholds()
T5_FAR = T5_THRESHOLDS[-1]


def _t5_bucket_static(d):
    b = 0
    for i, t in enumerate(T5_THRESHOLDS):
        if d >= t:
            b = i
    return b


def _mm_body(x_ref, w_ref, *o_refs):
    acc = jnp.dot(x_ref[...], w_ref[...], preferred_element_type=F32)
    for o in o_refs:
        o[...] = acc.astype(o.dtype)


def _matmul(x, w, out_dtypes, tm, tn, name):
    m, k = x.shape
    n = w.shape[1]
    tm, tn = min(tm, m), min(tn, n)
    return pl.pallas_call(
        _mm_body,
        grid=(m // tm, n // tn),
        in_specs=[pl.BlockSpec((tm, k), lambda i, j: (i, 0)),
                  pl.BlockSpec((k, tn), lambda i, j: (0, j))],
        out_specs=[pl.BlockSpec((tm, tn), lambda i, j: (i, j)) for _ in out_dtypes],
        out_shape=[jax.ShapeDtypeStruct((m, n), d) for d in out_dtypes],
        compiler_params=_cparams(2),
        name=name,
    )(x, w)


def _log_sigmoid(z):
    return jnp.minimum(z, 0.0) - jnp.log(1.0 + jnp.exp(-jnp.abs(z)))


def _logf_body(x_ref, w_ref, b_ref, lf_ref, cum_ref, carry, *, tt, cumulative):
    t = pl.program_id(1)
    x = x_ref[...]
    x_hi = x.astype(BF16)
    x_lo = (x - x_hi.astype(F32)).astype(BF16)
    z2 = jnp.dot(x_hi, w_ref[...], preferred_element_type=F32)
    z = (z2[:, 0:LANES] + z2[:, LANES:2 * LANES]
         + jnp.dot(x_lo, w_ref[:, 0:LANES], preferred_element_type=F32) + b_ref[...])
    lf = _log_sigmoid(z)
    lf_ref[...] = lf
    if cumulative:
        @pl.when(t == 0)
        def _():
            carry[...] = jnp.zeros_like(carry)

        row = lax.broadcasted_iota(jnp.int32, (tt, tt), 0)
        col = lax.broadcasted_iota(jnp.int32, (tt, tt), 1)
        tri = jnp.where(col <= row, 1.0, 0.0).astype(F32)
        cum = jnp.dot(tri, lf, precision=HIGHEST, preferred_element_type=F32) + carry[...]
        cum_ref[...] = cum
        carry[...] = cum[tt - 1:tt, :]
    else:
        cum_ref[...] = lf


def _log_forget(x32, w_f, b_f, n_seq, cumulative, name):
    m = x32.shape[0]
    t_len = m // n_seq
    tt = min(LOGF_TILE, t_len) if cumulative else m
    n_seq_grid = n_seq if cumulative else 1
    nt = (m // n_seq_grid) // tt
    spec = pl.BlockSpec((tt, LANES), lambda b, t: (b * nt + t, 0))
    return pl.pallas_call(
        functools.partial(_logf_body, tt=tt, cumulative=cumulative),
        grid=(n_seq_grid, nt),
        in_specs=[pl.BlockSpec((tt, D_MODEL), lambda b, t: (b * nt + t, 0)),
                  pl.BlockSpec((D_MODEL, 2 * LANES), lambda b, t: (0, 0)),
                  pl.BlockSpec((1, LANES), lambda b, t: (0, 0))],
        out_specs=[spec, spec],
        out_shape=[jax.ShapeDtypeStruct((m, LANES), F32)] * 2,
        scratch_shapes=[pltpu.VMEM((1, LANES), F32)],
        compiler_params=_cparams(2),
        name=name,
    )(x32, w_f, b_f)


def _layer_norm_rows(y, g, b):
    mean = jnp.mean(y, axis=1, keepdims=True)
    yc = y - mean
    var = jnp.mean(yc * yc, axis=1, keepdims=True)
    return yc * lax.rsqrt(var + LN_EPS) * g + b


def _outproj_ln_body(of_ref, om_ref, od_ref, w_ref, x_ref, g_ref, b_ref, o32_ref, ob_ref):
    nf, nm = H_FOX * HEAD_DIM, (H_FOX + H_MOBA) * HEAD_DIM
    a = jnp.dot(of_ref[...], w_ref[0:nf, :], preferred_element_type=F32)
    a += jnp.dot(om_ref[...], w_ref[nf:nm, :], preferred_element_type=F32)
    a += jnp.dot(od_ref[...], w_ref[nm:D_MODEL, :], preferred_element_type=F32)
    y = _layer_norm_rows(ALPHA * x_ref[...] + a, g_ref[...], b_ref[...])
    o32_ref[...] = y
    ob_ref[...] = y.astype(BF16)


def _outproj_ln(o_fox, o_moba, o_diff, w_out, x32, g, b, tm, name):
    m = x32.shape[0]
    tm = min(tm, m)
    row = lambda i: (i, 0)
    fixed = lambda i: (0, 0)
    out_specs = [pl.BlockSpec((tm, D_MODEL), row)] * 2
    out_shape = [jax.ShapeDtypeStruct((m, D_MODEL), F32), jax.ShapeDtypeStruct((m, D_MODEL), BF16)]
    return pl.pallas_call(
        _outproj_ln_body,
        grid=(m // tm,),
        in_specs=[pl.BlockSpec((tm, o_fox.shape[1]), row),
                  pl.BlockSpec((tm, o_moba.shape[1]), row),
                  pl.BlockSpec((tm, o_diff.shape[1]), row),
                  pl.BlockSpec((D_MODEL, D_MODEL), fixed, pipeline_mode=pl.Buffered(1)),
                  pl.BlockSpec((tm, D_MODEL), row),
                  pl.BlockSpec((1, D_MODEL), fixed),
                  pl.BlockSpec((1, D_MODEL), fixed)],
        out_specs=out_specs,
        out_shape=out_shape,
        compiler_params=_cparams(1),
        name=name,
    )(o_fox, o_moba, o_diff, w_out, x32, g, b)


def _swiglu_up_body(x_ref, wg_ref, wu_ref, h_ref):
    x = x_ref[...]
    g = jnp.dot(x, wg_ref[...], preferred_element_type=F32)
    u = jnp.dot(x, wu_ref[...], preferred_element_type=F32)
    h_ref[...] = (g * (1.0 / (1.0 + jnp.exp(-g))) * u).astype(h_ref.dtype)


def _swiglu_up(xb, wg, wu, tm, tn, name):
    m, k = xb.shape
    f = wg.shape[1]
    tm, tn = min(tm, m), min(tn, f)
    return pl.pallas_call(
        _swiglu_up_body,
        grid=(m // tm, f // tn),
        in_specs=[pl.BlockSpec((tm, k), lambda i, j: (i, 0)),
                  pl.BlockSpec((k, tn), lambda i, j: (0, j)),
                  pl.BlockSpec((k, tn), lambda i, j: (0, j))],
        out_specs=pl.BlockSpec((tm, tn), lambda i, j: (i, j)),
        out_shape=jax.ShapeDtypeStruct((m, f), BF16),
        compiler_params=_cparams(2),
        name=name,
    )(xb, wg, wu)


def _down_ln_body(h_ref, w_ref, x_ref, g_ref, b_ref, o32_ref, ob_ref, acc):
    kk = pl.program_id(1)

    @pl.when(kk == 0)
    def _():
        acc[...] = jnp.zeros_like(acc)

    acc[...] += jnp.dot(h_ref[...], w_ref[...], preferred_element_type=F32)

    @pl.when(kk == pl.num_programs(1) - 1)
    def _():
        y = _layer_norm_rows(ALPHA * x_ref[...] + acc[...], g_ref[...], b_ref[...])
        o32_ref[...] = y
        ob_ref[...] = y.astype(BF16)


def _down_ln(h, wd, x32, g, b, tm, tk, name):
    m, f = h.shape
    tm, tk = min(tm, m), min(tk, f)
    row = lambda i, k: (i, 0)
    fixed = lambda i, k: (0, 0)
    return pl.pallas_call(
        _down_ln_body,
        grid=(m // tm, f // tk),
        in_specs=[pl.BlockSpec((tm, tk), lambda i, k: (i, k)),
                  pl.BlockSpec((tk, D_MODEL), lambda i, k: (k, 0)),
                  pl.BlockSpec((tm, D_MODEL), row),
                  pl.BlockSpec((1, D_MODEL), fixed),
                  pl.BlockSpec((1, D_MODEL), fixed)],
        out_specs=[pl.BlockSpec((tm, D_MODEL), row)] * 2,
        out_shape=[jax.ShapeDtypeStruct((m, D_MODEL), F32), jax.ShapeDtypeStruct((m, D_MODEL), BF16)],
        scratch_shapes=[pltpu.VMEM((tm, D_MODEL), F32)],
        compiler_params=_cparams(2),
        name=name,
    )(h, wd, x32, g, b)


def _online_softmax_step(s, v, m_sc, l_sc, acc_sc):
    m_prev = m_sc[...]
    m_new = jnp.maximum(m_prev, jnp.max(s, axis=1, keepdims=True))
    alpha = jnp.exp(m_prev - m_new)
    p = jnp.exp(s - jnp.tile(m_new, (1, s.shape[1] // LANES)))
    l_sc[...] = alpha * l_sc[...] + jnp.sum(p, axis=1, keepdims=True)
    acc_sc[...] = alpha * acc_sc[...] + jnp.dot(p.astype(BF16), v, preferred_element_type=F32)
    m_sc[...] = m_new


def _softmax_scratch(tq):
    return [pltpu.VMEM((tq, LANES), F32), pltpu.VMEM((tq, LANES), F32), pltpu.VMEM((tq, HEAD_DIM), F32)]


def _init_softmax_state(m_sc, l_sc, acc_sc):
    m_sc[...] = jnp.full(m_sc.shape, NEG_INF, F32)
    l_sc[...] = jnp.zeros_like(l_sc)
    acc_sc[...] = jnp.zeros_like(acc_sc)


def _qk(q, k):
    return lax.dot_general(q, k, (((1,), (1,)), ((), ())), preferred_element_type=F32)


def _lane_column(x, idx):
    lane = lax.broadcasted_iota(jnp.int32, x.shape, 1)
    return jnp.sum(jnp.where(lane == idx, x, 0.0), axis=1, keepdims=True)


def _fox_prompt_body(q_ref, k_ref, v_ref, frow_ref, fcol_ref, o_ref, m_sc, l_sc, acc_sc, *, tq):
    h = pl.program_id(1)
    qi = pl.program_id(2)
    scale = HEAD_DIM ** -0.5
    q = q_ref[...]
    frow = jnp.broadcast_to(_lane_column(frow_ref[...], h), (tq, LANES))
    _init_softmax_state(m_sc, l_sc, acc_sc)

    def step(c, diagonal):
        start = pl.multiple_of(c * tq, tq)
        s = _qk(q, k_ref[pl.ds(start, tq), :]) * scale
        s = s + (jnp.tile(frow, (1, tq // LANES)) - fcol_ref[0, 0, :, pl.ds(start, tq)])
        if diagonal:
            row = lax.broadcasted_iota(jnp.int32, (tq, tq), 0)
            col = lax.broadcasted_iota(jnp.int32, (tq, tq), 1)
            s = jnp.where(col <= row, s, NEG_INF)
        _online_softmax_step(s, v_ref[pl.ds(start, tq), :], m_sc, l_sc, acc_sc)

    step(qi, True)

    def past(c, carry):
        step(c, False)
        return carry

    lax.fori_loop(0, qi, past, 0)
    o_ref[...] = (acc_sc[...] / l_sc[...]).astype(o_ref.dtype)


def _fox_prompt(q_maps, kb, vb, f_rows, f_cols, n_seq, name):
    m = q_maps.shape[0]
    t_len = m // n_seq
    tq = min(ATT_TILE, t_len)
    nq = t_len // tq
    return pl.pallas_call(
        functools.partial(_fox_prompt_body, tq=tq),
        grid=(n_seq, H_FOX, nq),
        in_specs=[pl.BlockSpec((tq, HEAD_DIM), lambda b, h, i: (b * nq + i, h)),
                  pl.BlockSpec((t_len, HEAD_DIM), lambda b, h, i: (b, h)),
                  pl.BlockSpec((t_len, HEAD_DIM), lambda b, h, i: (b, h)),
                  pl.BlockSpec((tq, LANES), lambda b, h, i: (b * nq + i, 0)),
                  pl.BlockSpec((1, 1, 1, t_len), lambda b, h, i: (b, h, 0, 0))],
        out_specs=pl.BlockSpec((tq, HEAD_DIM), lambda b, h, i: (b * nq + i, h)),
        out_shape=jax.ShapeDtypeStruct((m, H_FOX * HEAD_DIM), BF16),
        scratch_shapes=_softmax_scratch(tq),
        compiler_params=_cparams(3),
        name=name,
    )(q_maps, kb, vb, f_rows, f_cols)


def _bias_tiles_body(rel_ref, o_ref, *, tq):
    h = pl.program_id(0)
    row = lax.broadcasted_iota(jnp.int32, (tq, tq), 0)
    col = lax.broadcasted_iota(jnp.int32, (tq, tq), 1)
    for kind in range(2):
        d = row - col + kind * tq
        bias = jnp.full((tq, tq), rel_ref[h, 0], F32)
        for bkt in range(1, N_BUCKETS):
            bias = jnp.where(d >= T5_THRESHOLDS[bkt], rel_ref[h, bkt], bias)
        if kind == 0:
            bias = jnp.where(d >= 0, bias, NEG_INF)
        o_ref[0, kind] = bias


def _bias_tiles(rel_t, tq):
    nh = rel_t.shape[0]
    return pl.pallas_call(
        functools.partial(_bias_tiles_body, tq=tq),
        grid=(nh,),
        in_specs=[pl.BlockSpec(memory_space=pltpu.SMEM)],
        out_specs=pl.BlockSpec((1, 2, tq, tq), lambda h: (h, 0, 0, 0)),
        out_shape=jax.ShapeDtypeStruct((nh, 2, tq, tq), F32),
        compiler_params=_cparams(1),
        name="t5_bias_tiles",
    )(rel_t)


def _block_means_body(k_ref, o_ref, *, n_blocks):
    k = k_ref[...]
    o_ref[...] = jnp.zeros_like(o_ref)
    km = jnp.mean(k.reshape(n_blocks, MOBA_BLOCK, HEAD_DIM), axis=1)
    o_ref[0, 0, 0:n_blocks, :] = km


def _block_means(k32, n_seq, name):
    m = k32.shape[0]
    t_len = m // n_seq
    n_blocks = t_len // MOBA_BLOCK
    return pl.pallas_call(
        functools.partial(_block_means_body, n_blocks=n_blocks),
        grid=(n_seq, H_MOBA),
        in_specs=[pl.BlockSpec((t_len, HEAD_DIM), lambda b, h: (b, H_FOX + h))],
        out_specs=pl.BlockSpec((1, 1, LANES, HEAD_DIM), lambda b, h: (b, h, 0, 0)),
        out_shape=jax.ShapeDtypeStruct((n_seq, H_MOBA, LANES, HEAD_DIM), F32),
        compiler_params=_cparams(2),
        name=name,
    )(k32)


def _top_blocks(gate, n_valid, n_candidates, n_sel):
    lane = lax.broadcasted_iota(jnp.int32, gate.shape, 1)
    rank = jnp.zeros(gate.shape, F32)
    for c2 in range(n_candidates):
        g2 = gate[:, c2:c2 + 1]
        beats = jnp.where(g2 > gate, 1.0, jnp.where(g2 == gate, jnp.where(lane > c2, 1.0, 0.0), 0.0))
        rank = rank + beats * jnp.where(c2 < n_valid, 1.0, 0.0)
    return jnp.where(rank < n_sel, jnp.where(lane < n_valid, 1.0, 0.0), 0.0)


def _moba_prompt_body(far_ref, q_ref, k_ref, v_ref, means_ref, bias_ref, o_ref, m_sc, l_sc, acc_sc, mask_sc,
                      *, tq, t_len, n_candidates, n_sel):
    h = pl.program_id(1)
    qi = pl.program_id(2)
    scale = HEAD_DIM ** -0.5
    q = q_ref[...]
    gate = lax.dot_general(q.astype(F32), means_ref[0, 0], (((1,), (1,)), ((), ())),
                           precision=HIGHEST, preferred_element_type=F32)
    row = lax.broadcasted_iota(jnp.int32, (tq, LANES), 0)
    lane = lax.broadcasted_iota(jnp.int32, (tq, LANES), 1)
    own = qi * (tq // MOBA_BLOCK) + lax.div(row, MOBA_BLOCK)
    sel = jnp.where(lane == own, 1.0, _top_blocks(gate, own, n_candidates, n_sel))
    for b in range(t_len // MOBA_BLOCK):
        slab = jnp.where(sel[:, b:b + 1] > 0.5, 0.0, NEG_INF)
        mask_sc[:, b * LANES:(b + 1) * LANES] = jnp.broadcast_to(slab, (tq, LANES))
    _init_softmax_state(m_sc, l_sc, acc_sc)
    blocks_per_chunk = tq // MOBA_BLOCK

    def step(c, bias):
        start = pl.multiple_of(c * tq, tq)
        slabs = mask_sc[:, pl.ds(pl.multiple_of(c * blocks_per_chunk * LANES, blocks_per_chunk * LANES),
                                 blocks_per_chunk * LANES)]
        mask = jnp.concatenate([jnp.tile(slabs[:, j * LANES:(j + 1) * LANES], (1, MOBA_BLOCK // LANES))
                                for j in range(blocks_per_chunk)], axis=1)
        s = _qk(q, k_ref[pl.ds(start, tq), :]) * scale + bias + mask
        _online_softmax_step(s, v_ref[pl.ds(start, tq), :], m_sc, l_sc, acc_sc)

    step(qi, bias_ref[0, 0])

    @pl.when(qi > 0)
    def _():
        step(qi - 1, bias_ref[0, 1])

    far = far_ref[h]

    def past(c, carry):
        step(c, far)
        return carry

    lax.fori_loop(0, jnp.maximum(qi - 1, 0), past, 0)
    o_ref[...] = (acc_sc[...] / l_sc[...]).astype(o_ref.dtype)


def _moba_prompt(q_maps, kb, vb, means, bias_tiles, far, n_seq, name):
    m = q_maps.shape[0]
    t_len = m // n_seq
    tq = min(ATT_TILE, t_len)
    nq = t_len // tq
    n_candidates = max((t_len - 1) // MOBA_BLOCK, 1)
    n_sel = min(MOBA_TOPK, n_candidates)
    grid_spec = pltpu.PrefetchScalarGridSpec(
        num_scalar_prefetch=1,
        grid=(n_seq, H_MOBA, nq),
        in_specs=[pl.BlockSpec((tq, HEAD_DIM), lambda b, h, i, far: (b * nq + i, H_FOX + h)),
                  pl.BlockSpec((t_len, HEAD_DIM), lambda b, h, i, far: (b, H_FOX + h)),
                  pl.BlockSpec((t_len, HEAD_DIM), lambda b, h, i, far: (b, H_FOX + h)),
                  pl.BlockSpec((1, 1, LANES, HEAD_DIM), lambda b, h, i, far: (b, h, 0, 0)),
                  pl.BlockSpec((1, 2, tq, tq), lambda b, h, i, far: (h, 0, 0, 0))],
        out_specs=pl.BlockSpec((tq, HEAD_DIM), lambda b, h, i, far: (b * nq + i, h)),
        scratch_shapes=_softmax_scratch(tq) + [pltpu.VMEM((tq, (t_len // MOBA_BLOCK) * LANES), F32)],
    )
    return pl.pallas_call(
        functools.partial(_moba_prompt_body, tq=tq, t_len=t_len, n_candidates=n_candidates, n_sel=n_sel),
        grid_spec=grid_spec,
        out_shape=jax.ShapeDtypeStruct((m, H_MOBA * HEAD_DIM), BF16),
        compiler_params=_cparams(3),
        name=name,
    )(far, q_maps, kb, vb, means, bias_tiles)


def _diff_lambda(lam_ref, lam_init):
    r = lam_ref[...]
    s1 = jnp.sum(r[0:1, :] * r[1:2, :], axis=1, keepdims=True)
    s2 = jnp.sum(r[2:3, :] * r[3:4, :], axis=1, keepdims=True)
    return jnp.exp(s1) - jnp.exp(s2) + lam_init


def _diff_finish(o0, o1, lam, g, lam_init):
    o = o0 - lam * o1
    return o * lax.rsqrt(jnp.mean(o * o, axis=1, keepdims=True) + LN_EPS) * g * (1.0 - lam_init)


def _diff_prompt_body(far_ref, q_ref, k_ref, v_ref, bias_ref, lam_ref, g_ref, o_ref,
                      m0, l0, a0, m1, l1, a1, *, tq, lam_init):
    h = pl.program_id(1)
    qi = pl.program_id(2)
    scale = DIFF_QK_DIM ** -0.5
    q0 = q_ref[:, 0:HEAD_DIM]
    q1 = q_ref[:, HEAD_DIM:2 * HEAD_DIM]
    _init_softmax_state(m0, l0, a0)
    _init_softmax_state(m1, l1, a1)

    def step(c, bias):
        start = pl.multiple_of(c * tq, tq)
        k = k_ref[pl.ds(start, tq), :]
        v = v_ref[pl.ds(start, tq), :]
        _online_softmax_step(_qk(q0, k) * scale + bias, v, m0, l0, a0)
        _online_softmax_step(_qk(q1, k) * scale + bias, v, m1, l1, a1)

    step(qi, bias_ref[0, 0])

    @pl.when(qi > 0)
    def _():
        step(qi - 1, bias_ref[0, 1])

    far = far_ref[H_MOBA + h]

    def past(c, carry):
        step(c, far)
        return carry

    lax.fori_loop(0, jnp.maximum(qi - 1, 0), past, 0)
    lam = _diff_lambda(lam_ref, lam_init)
    o = _diff_finish(a0[...] / l0[...], a1[...] / l1[...], lam, g_ref[...], lam_init)
    o_ref[...] = o.astype(o_ref.dtype)


def _diff_prompt(q_maps, kb, vb, bias_tiles, far, lam_rows, g, lam_init, n_seq, name):
    m = q_maps.shape[0]
    t_len = m // n_seq
    tq = min(ATT_TILE, t_len)
    nq = t_len // tq
    h0 = H_FOX + H_MOBA
    grid_spec = pltpu.PrefetchScalarGridSpec(
        num_scalar_prefetch=1,
        grid=(n_seq, H_DIFF, nq),
        in_specs=[pl.BlockSpec((tq, 2 * HEAD_DIM), lambda b, h, i, far: (b * nq + i, h0 // 2 + h)),
                  pl.BlockSpec((t_len, HEAD_DIM), lambda b, h, i, far: (b, h0 + h)),
                  pl.BlockSpec((t_len, HEAD_DIM), lambda b, h, i, far: (b, h0 + h)),
                  pl.BlockSpec((1, 2, tq, tq), lambda b, h, i, far: (H_MOBA + h, 0, 0, 0)),
                  pl.BlockSpec((8, LANES), lambda b, h, i, far: (0, 0)),
                  pl.BlockSpec((1, HEAD_DIM), lambda b, h, i, far: (0, 0))],
        out_specs=pl.BlockSpec((tq, HEAD_DIM), lambda b, h, i, far: (b * nq + i, h)),
        scratch_shapes=_softmax_scratch(tq) + _softmax_scratch(tq),
    )
    return pl.pallas_call(
        functools.partial(_diff_prompt_body, tq=tq, lam_init=lam_init),
        grid_spec=grid_spec,
        out_shape=jax.ShapeDtypeStruct((m, H_DIFF * HEAD_DIM), BF16),
        compiler_params=_cparams(3),
        name=name,
    )(far, q_maps, kb, vb, bias_tiles, lam_rows, g)


def _sample_page_body(pt_ref, q_ref, *refs):
    g = SAMPLE_PAGES_PER_STEP
    k_refs, v_refs, sfx_refs = refs[0:g], refs[g:2 * g], refs[2 * g:3 * g]
    cb_ref, m_ref, l_ref, acc_ref, ksum_ref = refs[3 * g:]
    n_cols = PAGE_SIZE * N_HEADS
    row = lax.broadcasted_iota(jnp.int32, (SAMPLE_ROWS, n_cols), 0)
    col = lax.broadcasted_iota(jnp.int32, (SAMPLE_ROWS, n_cols), 1)
    scale = jnp.where(row < H_FOX + H_MOBA, HEAD_DIM ** -0.5, DIFF_QK_DIM ** -0.5).astype(F32)
    same_head = (col & (N_HEADS - 1)) == (row & (N_HEADS - 1))
    q = q_ref[0]
    for j in range(g):
        kp = k_refs[j][0, 0]
        kf = kp.reshape(n_cols, HEAD_DIM).astype(BF16)
        vf = v_refs[j][0, 0].reshape(n_cols, HEAD_DIM).astype(BF16)
        s = _qk(q, kf) * scale + (sfx_refs[j][0, 0] + cb_ref[0, j])
        s = jnp.where(same_head, s, NEG_INF)
        m = jnp.max(s, axis=1, keepdims=True)
        p = jnp.exp(s - m)
        l = jnp.sum(p, axis=1, keepdims=True)
        acc_ref[0, j] = jnp.dot(p.astype(BF16), vf, preferred_element_type=F32)
        m_ref[0, j] = jnp.broadcast_to(m, (SAMPLE_ROWS, LANES))
        l_ref[0, j] = jnp.broadcast_to(l, (SAMPLE_ROWS, LANES))
        ksum_ref[0, j] = jnp.sum(kp, axis=0)


def _sample_pages(layer, page_table, q_rows, cache_k, cache_v, sfx_rows, const_rows, name):
    n_seq, n_pages = page_table.shape
    g = SAMPLE_PAGES_PER_STEP
    n_cols = PAGE_SIZE * N_HEADS
    page_block = (1, 1, PAGE_SIZE, N_HEADS, HEAD_DIM)

    def page_spec(j):
        return pl.BlockSpec(page_block, lambda b, p, pt: (layer, pt[b, p * g + j], 0, 0, 0))

    def sfx_spec(j):
        return pl.BlockSpec((1, 1, 1, n_cols), lambda b, p, pt: (0, pt[b, p * g + j], 0, 0))

    stat_spec = pl.BlockSpec((1, g, SAMPLE_ROWS, LANES), lambda b, p, pt: (b, p, 0, 0))
    grid_spec = pltpu.PrefetchScalarGridSpec(
        num_scalar_prefetch=1,
        grid=(n_seq, n_pages // g),
        in_specs=([pl.BlockSpec((1, SAMPLE_ROWS, HEAD_DIM), lambda b, p, pt: (b, 0, 0))]
                  + [page_spec(j) for j in range(g)] + [page_spec(j) for j in range(g)]
                  + [sfx_spec(j) for j in range(g)]
                  + [pl.BlockSpec((1, g, 1, n_cols), lambda b, p, pt: (b, p, 0, 0))]),
        out_specs=[stat_spec, stat_spec, stat_spec,
                   pl.BlockSpec((1, g, N_HEADS, HEAD_DIM), lambda b, p, pt: (b, p, 0, 0))],
    )
    stat_shape = jax.ShapeDtypeStruct((n_seq, n_pages, SAMPLE_ROWS, LANES), F32)
    return pl.pallas_call(
        _sample_page_body,
        grid_spec=grid_spec,
        out_shape=[stat_shape, stat_shape, stat_shape,
                   jax.ShapeDtypeStruct((n_seq, n_pages, N_HEADS, HEAD_DIM), F32)],
        compiler_params=_cparams(2),
        name=name,
    )(page_table, q_rows, *([cache_k] * g), *([cache_v] * g), *([sfx_rows] * g), const_rows)


def _sample_combine_body(m_ref, l_ref, acc_ref, ksum_ref, q_ref, k_ref, v_ref, sb_ref, lam_ref, g_ref, o_ref,
                         *, n_pages, lam_init):
    n_blocks = n_pages * PAGE_SIZE // MOBA_BLOCK
    pages_per_block = MOBA_BLOCK // PAGE_SIZE
    n_sel = min(MOBA_TOPK, n_blocks)
    m = m_ref[0]
    l = l_ref[0]
    acc = acc_ref[0]
    q = q_ref[0]
    head = lax.broadcasted_iota(jnp.int32, (N_HEADS, LANES), 0)
    is_moba = jnp.where(head >= H_FOX, jnp.where(head < H_FOX + H_MOBA, 1.0, 0.0), 0.0)

    gp = jnp.sum(ksum_ref[0] * q[0:N_HEADS][None], axis=2, keepdims=True)
    gb = jnp.sum(gp.reshape(n_blocks, pages_per_block, N_HEADS, 1), axis=1) * (1.0 / MOBA_BLOCK)
    gb = jnp.broadcast_to(gb, (n_blocks, N_HEADS, LANES))
    blk = lax.broadcasted_iota(jnp.int32, (n_blocks, N_HEADS, LANES), 0)
    rank = jnp.zeros((n_blocks, N_HEADS, LANES), F32)
    for c2 in range(n_blocks):
        g2 = gb[c2:c2 + 1]
        rank = rank + jnp.where(g2 > gb, 1.0, jnp.where(g2 == gb, jnp.where(blk > c2, 1.0, 0.0), 0.0))
    sel_b = jnp.where(rank < n_sel, 1.0, 0.0)
    sel_b = jnp.maximum(sel_b, 1.0 - is_moba[None])
    sel_p = jnp.broadcast_to(sel_b[:, None], (n_blocks, pages_per_block, N_HEADS, LANES))
    sel_p = sel_p.reshape(n_pages, N_HEADS, LANES)
    sel = jnp.concatenate([sel_p, jnp.ones_like(sel_p)], axis=1)

    row = lax.broadcasted_iota(jnp.int32, (SAMPLE_ROWS, 1), 0)
    scale = jnp.where(row < H_FOX + H_MOBA, HEAD_DIM ** -0.5, DIFF_QK_DIM ** -0.5).astype(F32)
    s_self = jnp.sum(q * k_ref[0], axis=1, keepdims=True) * scale + sb_ref[0]

    m_sel = jnp.where(sel > 0.0, m, NEG_INF)
    m_all = jnp.maximum(jnp.max(m_sel, axis=0), s_self)
    w = jnp.where(sel > 0.0, jnp.exp(m_sel - m_all[None]), 0.0)
    w_self = jnp.exp(s_self - m_all)
    den = jnp.sum(w * l, axis=0) + w_self
    num = jnp.sum(w * acc, axis=0) + w_self * v_ref[0]
    o = num / den
    o_first, o_second = o[0:N_HEADS], o[N_HEADS:SAMPLE_ROWS]
    lam = _diff_lambda(lam_ref, lam_init)
    od = _diff_finish(o_first, o_second, lam, g_ref[...], lam_init)
    o_ref[0] = jnp.where(head >= H_FOX + H_MOBA, od, o_first).astype(o_ref.dtype)


def _sample_combine(m, l, acc, ksum, q_rows32, k_rows, v_rows, self_bias, lam_rows, g, lam_init, name):
    n_seq, n_pages = m.shape[:2]
    stat_spec = pl.BlockSpec((1, n_pages, SAMPLE_ROWS, LANES), lambda b: (b, 0, 0, 0))
    row_spec = pl.BlockSpec((1, SAMPLE_ROWS, HEAD_DIM), lambda b: (b, 0, 0))
    return pl.pallas_call(
        functools.partial(_sample_combine_body, n_pages=n_pages, lam_init=lam_init),
        grid=(n_seq,),
        in_specs=[stat_spec, stat_spec, stat_spec,
                  pl.BlockSpec((1, n_pages, N_HEADS, HEAD_DIM), lambda b: (b, 0, 0, 0)),
                  row_spec, row_spec, row_spec,
                  pl.BlockSpec((1, SAMPLE_ROWS, LANES), lambda b: (0, 0, 0)),
                  pl.BlockSpec((8, LANES), lambda b: (0, 0)),
                  pl.BlockSpec((1, HEAD_DIM), lambda b: (0, 0))],
        out_specs=pl.BlockSpec((1, N_HEADS, HEAD_DIM), lambda b: (b, 0, 0)),
        out_shape=jax.ShapeDtypeStruct((n_seq, N_HEADS, HEAD_DIM), BF16),
        compiler_params=_cparams(1),
        name=name,
    )(m, l, acc, ksum, q_rows32, k_rows, v_rows, self_bias, lam_rows, g)


def _router_body(x_ref, w_ref, b_ref, o_ref):
    logits = jnp.dot(x_ref[...], w_ref[...], precision=HIGHEST, preferred_element_type=F32) + b_ref[...]
    lane = lax.broadcasted_iota(jnp.int32, logits.shape, 1)
    lane_f = lane.astype(F32)
    logits = jnp.where(lane < N_EXPERTS, logits, NEG_INF)
    v1 = jnp.max(logits, axis=1, keepdims=True)
    i1 = jnp.min(jnp.where(logits == v1, lane_f, float(LANES)), axis=1, keepdims=True)
    rest = jnp.where(lane_f == i1, NEG_INF, logits)
    v2 = jnp.max(rest, axis=1, keepdims=True)
    i2 = jnp.min(jnp.where(rest == v2, lane_f, float(LANES)), axis=1, keepdims=True)
    e2 = jnp.exp(v2 - v1)
    g1 = 1.0 / (1.0 + e2)
    g2 = e2 / (1.0 + e2)
    out = jnp.where(lane == 0, i1, jnp.where(lane == 1, i2, jnp.where(lane == 2, g1, jnp.where(lane == 3, g2, 0.0))))
    o_ref[...] = out


def _router(x32, w_r, b_r, tm, name):
    m = x32.shape[0]
    tm = min(tm, m)
    return pl.pallas_call(
        _router_body,
        grid=(m // tm,),
        in_specs=[pl.BlockSpec((tm, D_MODEL), lambda i: (i, 0)),
                  pl.BlockSpec((D_MODEL, LANES), lambda i: (0, 0)),
                  pl.BlockSpec((1, LANES), lambda i: (0, 0))],
        out_specs=pl.BlockSpec((tm, LANES), lambda i: (i, 0)),
        out_shape=jax.ShapeDtypeStruct((m, LANES), F32),
        compiler_params=_cparams(1),
        name=name,
    )(x32, w_r, b_r)


def _row_copy(src_hbm, idx, dst, r, sem):
    return pltpu.make_async_copy(src_hbm.at[pl.ds(idx, 1), :], dst.at[pl.ds(r, 1), :], sem)


def _start_row_gather(src_hbm, idx_ref, dst, sem, n_rows):
    def start(r0, carry):
        for u in range(GATHER_UNROLL):
            r = r0 * GATHER_UNROLL + u
            _row_copy(src_hbm, idx_ref[0, 0, r], dst, r, sem).start(priority=u % 2)
        return carry

    lax.fori_loop(0, n_rows // GATHER_UNROLL, start, 0)


def _wait_row_gather(src_hbm, dst, sem, n_rows):
    pltpu.make_async_copy(src_hbm.at[pl.ds(0, n_rows), :], dst, sem).wait()


def _expert_up_body(te_ref, nu_ref, idx_ref, idx_next_ref, x_hbm, wg_ref, wu_ref, h_ref, rows, xb, sems, *, tm):
    i = pl.program_id(0)
    n_used = nu_ref[0]
    slot = lax.rem(i, 2)

    @pl.when(jnp.logical_and(pl.program_id(1) == 0, i < n_used))
    def _():
        @pl.when(i == 0)
        def _():
            _start_row_gather(x_hbm, idx_ref, rows.at[0], sems.at[0], tm)

        _wait_row_gather(x_hbm, rows.at[slot], sems.at[slot], tm)

        @pl.when(i + 1 < n_used)
        def _():
            _start_row_gather(x_hbm, idx_next_ref, rows.at[1 - slot], sems.at[1 - slot], tm)

        xb[...] = rows[slot].astype(BF16)

    @pl.when(i < n_used)
    def _():
        x = xb[...]
        g = jnp.dot(x, wg_ref[0], preferred_element_type=F32)
        u = jnp.dot(x, wu_ref[0], preferred_element_type=F32)
        h_ref[...] = (g * (1.0 / (1.0 + jnp.exp(-g))) * u).astype(h_ref.dtype)

    @pl.when(i >= n_used)
    def _():
        h_ref[...] = jnp.zeros_like(h_ref)


def _expert_up(x32, row_token, wg, wu, tile_expert, n_used, tm, tn, name):
    p = row_token.shape[0]
    k = x32.shape[1]
    f = wg.shape[2]
    tn = min(tn, f)
    nf = f // tn
    nt = p // tm
    idx = row_token.reshape(nt, 1, tm)
    wspec = pl.BlockSpec((1, k, tn), lambda i, j, te, nu: (te[i], 0, jnp.where(i < nu[0], j, nf - 1)))
    grid_spec = pltpu.PrefetchScalarGridSpec(
        num_scalar_prefetch=2,
        grid=(nt, nf),
        in_specs=[pl.BlockSpec((1, 1, tm), lambda i, j, te, nu: (i, 0, 0), memory_space=pltpu.SMEM),
                  pl.BlockSpec((1, 1, tm), lambda i, j, te, nu: (jnp.minimum(i + 1, nt - 1), 0, 0),
                               memory_space=pltpu.SMEM),
                  pl.BlockSpec(memory_space=pl.ANY), wspec, wspec],
        out_specs=pl.BlockSpec((tm, tn), lambda i, j, te, nu: (i, j)),
        scratch_shapes=[pltpu.VMEM((2, tm, k), F32), pltpu.VMEM((tm, k), BF16), pltpu.SemaphoreType.DMA((2,))],
    )
    return pl.pallas_call(
        functools.partial(_expert_up_body, tm=tm),
        grid_spec=grid_spec,
        out_shape=jax.ShapeDtypeStruct((p, f), BF16),
        compiler_params=_cparams(2),
        name=name,
    )(tile_expert, n_used, idx, idx, x32, wg, wu)


def _expert_down_body(te_ref, nu_ref, h_ref, w_ref, y_ref, acc):
    kk = pl.program_id(1)

    @pl.when(pl.program_id(0) < nu_ref[0])
    def _():
        @pl.when(kk == 0)
        def _():
            acc[...] = jnp.zeros_like(acc)

        acc[...] += jnp.dot(h_ref[...], w_ref[0], preferred_element_type=F32)

        @pl.when(kk == pl.num_programs(1) - 1)
        def _():
            y_ref[...] = acc[...]

    @pl.when(pl.program_id(0) >= nu_ref[0])
    def _():
        y_ref[...] = jnp.zeros_like(y_ref)


def _expert_down(h, wd, tile_expert, n_used, tm, tk, name):
    p, f = h.shape
    tk = min(tk, f)
    nk = f // tk
    grid_spec = pltpu.PrefetchScalarGridSpec(
        num_scalar_prefetch=2,
        grid=(p // tm, nk),
        in_specs=[pl.BlockSpec((tm, tk), lambda i, k, te, nu: (i, k)),
                  pl.BlockSpec((1, tk, D_MODEL), lambda i, k, te, nu: (te[i], jnp.where(i < nu[0], k, nk - 1), 0))],
        out_specs=pl.BlockSpec((tm, D_MODEL), lambda i, k, te, nu: (i, 0)),
        scratch_shapes=[pltpu.VMEM((tm, D_MODEL), F32)],
    )
    return pl.pallas_call(
        _expert_down_body,
        grid_spec=grid_spec,
        out_shape=jax.ShapeDtypeStruct((p, D_MODEL), F32),
        compiler_params=_cparams(2),
        name=name,
    )(tile_expert, n_used, h, wd)


def _moe_combine_body(p0_ref, p1_ref, y_hbm, r_ref, x_ref, g_ref, b_ref, o32_ref, ob_ref, buf0, buf1, sem0, sem1,
                      *, tr):
    _start_row_gather(y_hbm, p0_ref, buf0, sem0, tr)
    _start_row_gather(y_hbm, p1_ref, buf1, sem1, tr)
    _wait_row_gather(y_hbm, buf0, sem0, tr)
    _wait_row_gather(y_hbm, buf1, sem1, tr)
    r = r_ref[...]
    y = ALPHA * x_ref[...] + (r[:, 2:3] * buf0[...] + r[:, 3:4] * buf1[...])
    y = _layer_norm_rows(y, g_ref[...], b_ref[...])
    o32_ref[...] = y
    ob_ref[...] = y.astype(BF16)


def _moe_combine(y_sorted, pos0, pos1, routes, x32, g, b, tr, name):
    m = x32.shape[0]
    tr = min(tr, m)
    idx_spec = pl.BlockSpec((1, 1, tr), lambda i: (i, 0, 0), memory_space=pltpu.SMEM)
    row = lambda i: (i, 0)
    fixed = lambda i: (0, 0)
    return pl.pallas_call(
        functools.partial(_moe_combine_body, tr=tr),
        grid=(m // tr,),
        in_specs=[idx_spec, idx_spec,
                  pl.BlockSpec(memory_space=pl.ANY),
                  pl.BlockSpec((tr, LANES), row),
                  pl.BlockSpec((tr, D_MODEL), row),
                  pl.BlockSpec((1, D_MODEL), fixed),
                  pl.BlockSpec((1, D_MODEL), fixed)],
        out_specs=[pl.BlockSpec((tr, D_MODEL), row)] * 2,
        out_shape=[jax.ShapeDtypeStruct((m, D_MODEL), F32), jax.ShapeDtypeStruct((m, D_MODEL), BF16)],
        scratch_shapes=[pltpu.VMEM((tr, D_MODEL), F32), pltpu.VMEM((tr, D_MODEL), F32),
                        pltpu.SemaphoreType.DMA(()), pltpu.SemaphoreType.DMA(())],
        compiler_params=_cparams(1),
        name=name,
    )(pos0.reshape(m // tr, 1, tr), pos1.reshape(m // tr, 1, tr), y_sorted, routes, x32, g, b)


def _routing_plan(routes, tm):
    m = routes.shape[0]
    experts = jnp.concatenate([routes[:, 0], routes[:, 1]]).astype(jnp.int32)
    onehot = (experts[:, None] == jnp.arange(N_EXPERTS, dtype=jnp.int32)[None, :]).astype(jnp.int32)
    csum = jnp.cumsum(onehot, axis=0)
    rank = jnp.sum(csum * onehot, axis=1) - 1
    counts = csum[-1]
    padded = ((counts + tm - 1) // tm) * tm
    ends = jnp.cumsum(padded)
    starts = ends - padded
    pos = jnp.sum(onehot * starts[None, :], axis=1) + rank
    n_rows = (pl.cdiv(TOP_K * m, tm) + N_EXPERTS) * tm
    token = jnp.concatenate([jnp.arange(m, dtype=jnp.int32)] * TOP_K)
    row_token = jnp.zeros((n_rows,), jnp.int32).at[pos].set(token)
    tile_start = jnp.arange(n_rows // tm, dtype=jnp.int32) * tm
    tile_expert = jnp.minimum(jnp.sum((tile_start[:, None] >= ends[None, :]).astype(jnp.int32), axis=1),
                              N_EXPERTS - 1)
    n_used = (ends[-1] // tm).reshape(1).astype(jnp.int32)
    return row_token, pos[:m], pos[m:], tile_expert, n_used


def _moe_layer(x32, w_r, b_r, wg, wu, wd, g, b, tm, tn, tk, tag):
    routes = _router(x32, w_r, b_r, 512, f"router_{tag}")
    row_token, pos0, pos1, tile_expert, n_used = _routing_plan(routes, tm)
    h = _expert_up(x32, row_token, wg, wu, tile_expert, n_used, tm, tn, f"moe_up_{tag}")
    y = _expert_down(h, wd, tile_expert, n_used, tm, tk, f"moe_down_{tag}")
    return _moe_combine(y, pos0, pos1, routes, x32, g, b, 256, f"moe_combine_{tag}")


def _q_map_weights(w_q):
    h0 = (H_FOX + H_MOBA) * HEAD_DIM
    cols = [w_q[:, :h0]]
    zeros = jnp.zeros((w_q.shape[0], DIFF_QK_DIM), w_q.dtype)
    for i in range(H_DIFF):
        wh = w_q[:, h0 + i * HEAD_DIM:h0 + (i + 1) * HEAD_DIM]
        cols += [wh[:, :DIFF_QK_DIM], zeros, zeros, wh[:, DIFF_QK_DIM:]]
    return jnp.concatenate(cols, axis=1)


def _sample_row_order():
    first = list(range(H_FOX + H_MOBA)) + [H_FOX + H_MOBA + 2 * i for i in range(H_DIFF)]
    second = [H_FOX + H_MOBA + 2 * i + 1 for i in range(H_DIFF)]
    return first, second


def kernel(x_prompt, x_sample, cache_k, cache_v, cache_logf, page_table, w_in, b_forget, w_out, lam_q1, lam_k1, lam_q2, lam_k2, subln_g, rel_bias, ln1_g, ln1_b, ln2_g, ln2_b, w_ffn_gate, w_ffn_up, w_ffn_down, w_router, b_router, w_exp_gate, w_exp_up, w_exp_down):
    n_seq, t_len, _ = x_prompt.shape
    n_dec, dec_seq, _ = x_sample.shape
    assert dec_seq == 1 and t_len % MOBA_BLOCK == 0
    n_pages = page_table.shape[1]
    past = n_pages * PAGE_SIZE
    assert past % MOBA_BLOCK == 0 and past >= MOBA_BLOCK and n_pages % SAMPLE_PAGES_PER_STEP == 0
    depth = w_in.shape[0]
    n_pool = cache_k.shape[1]
    mp, ms = n_seq * t_len, n_dec
    n_cols = PAGE_SIZE * N_HEADS

    rel_t = jnp.transpose(rel_bias).astype(F32)
    far = rel_t[:, N_BUCKETS - 1]
    bias_tiles = _bias_tiles(rel_t, min(ATT_TILE, t_len))
    dist = past - (jnp.arange(n_pages)[:, None] * PAGE_SIZE + jnp.arange(PAGE_SIZE)[None, :])
    bucket_of = jnp.array([_t5_bucket_static(d) for d in range(T5_FAR + 1)], jnp.int32)
    bkt = bucket_of[jnp.minimum(dist, T5_FAR)]
    t5_cols = jnp.concatenate([jnp.zeros((n_pages, PAGE_SIZE, H_FOX), F32), rel_bias[bkt]], axis=2)
    first_rows, second_rows = _sample_row_order()
    self_bias16 = jnp.concatenate([jnp.zeros((H_FOX,), F32), rel_bias[0]])
    self_bias = jnp.broadcast_to(jnp.concatenate([self_bias16, self_bias16])[None, :, None],
                                 (1, SAMPLE_ROWS, LANES))

    xp32 = x_prompt.reshape(mp, D_MODEL)
    xs32 = x_sample.reshape(ms, D_MODEL)
    xpb, xsb = xp32.astype(BF16), xs32.astype(BF16)
    outs = {k: [] for k in ("kp", "vp", "fp", "ks", "vs", "fs")}

    for l in range(depth):
        lam_init = 0.8 - 0.6 * math.exp(-0.3 * l)
        w_q = _q_map_weights(w_in[l, :, :D_MODEL]).astype(BF16)
        w_k = w_in[l, :, D_MODEL:2 * D_MODEL].astype(BF16)
        w_v = w_in[l, :, 2 * D_MODEL:3 * D_MODEL].astype(BF16)
        w_f32 = jnp.pad(w_in[l, :, 3 * D_MODEL:], ((0, 0), (0, LANES - H_FOX)))
        w_f_hi = w_f32.astype(BF16)
        w_f = jnp.concatenate([w_f_hi, (w_f32 - w_f_hi.astype(F32)).astype(BF16)], axis=1)
        b_f = jnp.pad(b_forget[l], (0, LANES - H_FOX)).reshape(1, LANES)
        w_o = w_out[l].astype(BF16)
        lam_rows = jnp.pad(jnp.stack([lam_q1[l], lam_k1[l], lam_q2[l], lam_k2[l]]),
                           ((0, 4), (0, LANES - DIFF_QK_DIM)))
        g_sub = subln_g[l].reshape(1, HEAD_DIM)
        g1, b1 = ln1_g[l].reshape(1, D_MODEL), ln1_b[l].reshape(1, D_MODEL)
        g2, b2 = ln2_g[l].reshape(1, D_MODEL), ln2_b[l].reshape(1, D_MODEL)

        (qp,) = _matmul(xpb, w_q, [BF16], 1024, 1280, f"proj_q_p{l}")
        kp32, kpb = _matmul(xpb, w_k, [F32, BF16], 1024, 1024, f"proj_k_p{l}")
        vp32, vpb = _matmul(xpb, w_v, [F32, BF16], 1024, 1024, f"proj_v_p{l}")
        lfp, fcum = _log_forget(xp32, w_f, b_f, n_seq, True, f"logf_p{l}")
        f_cols = jnp.transpose(fcum.reshape(n_seq, t_len, LANES)[:, :, :8], (0, 2, 1)).reshape(n_seq, 8, 1, t_len)
        o_fox = _fox_prompt(qp, kpb, vpb, fcum, f_cols, n_seq, f"fox_p{l}")
        means = _block_means(kp32, n_seq, f"moba_means_p{l}")
        o_moba = _moba_prompt(qp, kpb, vpb, means, bias_tiles, far, n_seq, f"moba_p{l}")
        o_diff = _diff_prompt(qp, kpb, vpb, bias_tiles, far, lam_rows, g_sub, lam_init, n_seq, f"diff_p{l}")
        xp32, xpb = _outproj_ln(o_fox, o_moba, o_diff, w_o, xp32, g1, b1, 256, f"outproj_p{l}")

        (qs,) = _matmul(xsb, w_q, [F32], 512, 512, f"proj_q_s{l}")
        ks32, _ = _matmul(xsb, w_k, [F32, BF16], 512, 512, f"proj_k_s{l}")
        vs32, _ = _matmul(xsb, w_v, [F32, BF16], 512, 512, f"proj_v_s{l}")
        lfs, _ = _log_forget(xs32, w_f, b_f, n_dec, False, f"logf_s{l}")
        qs = qs.reshape(ms, N_QMAPS, HEAD_DIM)
        pad_rows = jnp.zeros((ms, SAMPLE_ROWS - N_HEADS - H_DIFF, HEAD_DIM), F32)
        q_rows32 = jnp.concatenate([qs[:, jnp.array(first_rows)], pad_rows, qs[:, jnp.array(second_rows)]], axis=1)
        k_heads = ks32.reshape(ms, N_HEADS, HEAD_DIM)
        v_heads = vs32.reshape(ms, N_HEADS, HEAD_DIM)
        k_rows = jnp.concatenate([k_heads, k_heads], axis=1)
        v_rows = jnp.concatenate([v_heads, v_heads], axis=1)
        lf_pool = cache_logf[l].astype(F32)
        pool_tot = jnp.sum(lf_pool, axis=1)
        sfx_pool = pool_tot[:, None, :] - jnp.cumsum(lf_pool, axis=1)
        sfx_rows = jnp.pad(sfx_pool, ((0, 0), (0, 0), (0, N_HEADS - H_FOX))).reshape(1, n_pool, 1, n_cols)
        page_tot = pool_tot[page_table]
        later = jnp.sum(page_tot, axis=1, keepdims=True) - jnp.cumsum(page_tot, axis=1)
        fox_const = later + lfs[:, None, :H_FOX]
        const16 = jnp.concatenate(
            [jnp.broadcast_to(fox_const[:, :, None, :], (ms, n_pages, PAGE_SIZE, H_FOX)),
             jnp.broadcast_to(t5_cols[None, :, :, H_FOX:], (ms, n_pages, PAGE_SIZE, N_HEADS - H_FOX))], axis=3)
        const_rows = const16.reshape(ms, n_pages, 1, n_cols)
        m_s, l_s, acc_s, ksum = _sample_pages(l, page_table, q_rows32.astype(BF16), cache_k, cache_v,
                                              sfx_rows, const_rows, f"sample_pages{l}")
        o_s = _sample_combine(m_s, l_s, acc_s, ksum, q_rows32, k_rows, v_rows, self_bias, lam_rows, g_sub,
                              lam_init, f"sample_combine{l}").reshape(ms, D_MODEL)
        nf, nm = H_FOX * HEAD_DIM, (H_FOX + H_MOBA) * HEAD_DIM
        xs32, xsb = _outproj_ln(o_s[:, :nf], o_s[:, nf:nm], o_s[:, nm:], w_o, xs32, g1, b1, 256, f"outproj_s{l}")

        i = l // 2
        if l % 2 == 0:
            wg, wu, wd = (w_ffn_gate[i].astype(BF16), w_ffn_up[i].astype(BF16), w_ffn_down[i].astype(BF16))
            hp = _swiglu_up(xpb, wg, wu, 1024, 512, f"ffn_up_p{l}")
            xp32, xpb = _down_ln(hp, wd, xp32, g2, b2, 512, 1408, f"ffn_down_p{l}")
            hs = _swiglu_up(xsb, wg, wu, 1024, 512, f"ffn_up_s{l}")
            xs32, xsb = _down_ln(hs, wd, xs32, g2, b2, 512, 1408, f"ffn_down_s{l}")
        else:
            wg, wu, wd = (w_exp_gate[i].astype(BF16), w_exp_up[i].astype(BF16), w_exp_down[i].astype(BF16))
            w_r = jnp.pad(w_router[i], ((0, 0), (0, LANES - N_EXPERTS)))
            b_r = jnp.pad(b_router[i], (0, LANES - N_EXPERTS)).reshape(1, LANES)
            xp32, xpb = _moe_layer(xp32, w_r, b_r, wg, wu, wd, g2, b2, 512, 512, 1408, f"p{l}")
            xs32, xsb = _moe_layer(xs32, w_r, b_r, wg, wu, wd, g2, b2, 16, 512, 1408, f"s{l}")

        outs["kp"].append(kp32.reshape(n_seq, t_len // PAGE_SIZE, PAGE_SIZE, N_HEADS, HEAD_DIM))
        outs["vp"].append(vp32.reshape(n_seq, t_len // PAGE_SIZE, PAGE_SIZE, N_HEADS, HEAD_DIM))
        outs["fp"].append(lfp[:, :H_FOX].reshape(n_seq, t_len // PAGE_SIZE, PAGE_SIZE, H_FOX))
        outs["ks"].append(ks32.reshape(n_dec, 1, N_HEADS, HEAD_DIM))
        outs["vs"].append(vs32.reshape(n_dec, 1, N_HEADS, HEAD_DIM))
        outs["fs"].append(lfs[:, :H_FOX].reshape(n_dec, 1, H_FOX))

    return (xp32.reshape(n_seq, t_len, D_MODEL), xs32.reshape(n_dec, 1, D_MODEL),
            jnp.stack(outs["kp"]), jnp.stack(outs["vp"]), jnp.stack(outs["fp"]),
            jnp.stack(outs["ks"]), jnp.stack(outs["vs"]), jnp.stack(outs["fs"]))
```

```python
import functools
import math

import jax
import jax.numpy as jnp
from jax import lax
from jax.experimental import pallas as pl
from jax.experimental.pallas import tpu as pltpu

F32 = jnp.float32
BF16 = jnp.bfloat16
HIGHEST = lax.Precision.HIGHEST
NEG_INF = float("-inf")

HEAD_DIM = 128
N_HEADS = 16
D_MODEL = N_HEADS * HEAD_DIM
H_FOX = 6
H_MOBA = 6
H_DIFF = 4
DIFF_QK_DIM = HEAD_DIM // 2
N_QMAPS = H_FOX + H_MOBA + 2 * H_DIFF
MOBA_BLOCK = 256
MOBA_TOPK = 3
PAGE_SIZE = 128
N_BUCKETS = 32
MAX_DISTANCE = 128
N_EXPERTS = 8
TOP_K = 2
DEPTH = 2
ALPHA = (2.0 * DEPTH) ** 0.25
LN_EPS = 1e-5

LANES = 128
VMEM_LIMIT_BYTES = 56 * 1024 * 1024
ATT_TILE = 2 * MOBA_BLOCK
LOGF_TILE = 256
SAMPLE_ROWS = 2 * N_HEADS
SAMPLE_PAGES_PER_STEP = 4
GATHER_UNROLL = 8


def _cparams(n_axes):
    return pltpu.CompilerParams(
        dimension_semantics=("arbitrary",) * n_axes, vmem_limit_bytes=VMEM_LIMIT_BYTES)


def _t5_thresholds():
    max_exact = N_BUCKETS // 2

    def bucket(d):
        if d < max_exact:
            return d
        r = math.log(d / max_exact) / math.log(MAX_DISTANCE / max_exact)
        return min(max_exact + int(r * (N_BUCKETS - max_exact)), N_BUCKETS - 1)

    return [min(d for d in range(2 * MAX_DISTANCE) if bucket(d) >= b) for b in range(N_BUCKETS)]


T5_THRESHOLDS = _t5_thresholds()
T5_FAR = T5_THRESHOLDS[-1]


def _t5_bucket_static(d):
    b = 0
    for i, t in enumerate(T5_THRESHOLDS):
        if d >= t:
            b = i
    return b


def _mm_body(x_ref, w_ref, *o_refs):
    acc = jnp.dot(x_ref[...], w_ref[...], preferred_element_type=F32)
    for o in o_refs:
        o[...] = acc.astype(o.dtype)


def _matmul(x, w, out_dtypes, tm, tn, name):
    m, k = x.shape
    n = w.shape[1]
    tm, tn = min(tm, m), min(tn, n)
    return pl.pallas_call(
        _mm_body,
        grid=(m // tm, n // tn),
        in_specs=[pl.BlockSpec((tm, k), lambda i, j: (i, 0)),
                  pl.BlockSpec((k, tn), lambda i, j: (0, j))],
        out_specs=[pl.BlockSpec((tm, tn), lambda i, j: (i, j)) for _ in out_dtypes],
        out_shape=[jax.ShapeDtypeStruct((m, n), d) for d in out_dtypes],
        compiler_params=_cparams(2),
        name=name,
    )(x, w)


def _log_sigmoid(z):
    return jnp.minimum(z, 0.0) - jnp.log(1.0 + jnp.exp(-jnp.abs(z)))


def _logf_body(x_ref, w_ref, b_ref, lf_ref, cum_ref, carry, *, tt, cumulative):
    t = pl.program_id(1)
    x = x_ref[...]
    x_hi = x.astype(BF16)
    x_lo = (x - x_hi.astype(F32)).astype(BF16)
    z2 = jnp.dot(x_hi, w_ref[...], preferred_element_type=F32)
    z = (z2[:, 0:LANES] + z2[:, LANES:2 * LANES]
         + jnp.dot(x_lo, w_ref[:, 0:LANES], preferred_element_type=F32) + b_ref[...])
    lf = _log_sigmoid(z)
    lf_ref[...] = lf
    if cumulative:
        @pl.when(t == 0)
        def _():
            carry[...] = jnp.zeros_like(carry)

        row = lax.broadcasted_iota(jnp.int32, (tt, tt), 0)
        col = lax.broadcasted_iota(jnp.int32, (tt, tt), 1)
        tri = jnp.where(col <= row, 1.0, 0.0).astype(F32)
        cum = jnp.dot(tri, lf, precision=HIGHEST, preferred_element_type=F32) + carry[...]
        cum_ref[...] = cum
        carry[...] = cum[tt - 1:tt, :]
    else:
        cum_ref[...] = lf


def _log_forget(x32, w_f, b_f, n_seq, cumulative, name):
    m = x32.shape[0]
    t_len = m // n_seq
    tt = min(LOGF_TILE, t_len) if cumulative else m
    n_seq_grid = n_seq if cumulative else 1
    nt = (m // n_seq_grid) // tt
    spec = pl.BlockSpec((tt, LANES), lambda b, t: (b * nt + t, 0))
    return pl.pallas_call(
        functools.partial(_logf_body, tt=tt, cumulative=cumulative),
        grid=(n_seq_grid, nt),
        in_specs=[pl.BlockSpec((tt, D_MODEL), lambda b, t: (b * nt + t, 0)),
                  pl.BlockSpec((D_MODEL, 2 * LANES), lambda b, t: (0, 0)),
                  pl.BlockSpec((1, LANES), lambda b, t: (0, 0))],
        out_specs=[spec, spec],
        out_shape=[jax.ShapeDtypeStruct((m, LANES), F32)] * 2,
        scratch_shapes=[pltpu.VMEM((1, LANES), F32)],
        compiler_params=_cparams(2),
        name=name,
    )(x32, w_f, b_f)


def _layer_norm_rows(y, g, b):
    mean = jnp.mean(y, axis=1, keepdims=True)
    yc = y - mean
    var = jnp.mean(yc * yc, axis=1, keepdims=True)
    return yc * lax.rsqrt(var + LN_EPS) * g + b


def _outproj_ln_body(of_ref, om_ref, od_ref, w_ref, x_ref, g_ref, b_ref, o32_ref, ob_ref):
    nf, nm = H_FOX * HEAD_DIM, (H_FOX + H_MOBA) * HEAD_DIM
    a = jnp.dot(of_ref[...], w_ref[0:nf, :], preferred_element_type=F32)
    a += jnp.dot(om_ref[...], w_ref[nf:nm, :], preferred_element_type=F32)
    a += jnp.dot(od_ref[...], w_ref[nm:D_MODEL, :], preferred_element_type=F32)
    y = _layer_norm_rows(ALPHA * x_ref[...] + a, g_ref[...], b_ref[...])
    o32_ref[...] = y
    ob_ref[...] = y.astype(BF16)


def _outproj_ln(o_fox, o_moba, o_diff, w_out, x32, g, b, tm, name):
    m = x32.shape[0]
    tm = min(tm, m)
    row = lambda i: (i, 0)
    fixed = lambda i: (0, 0)
    out_specs = [pl.BlockSpec((tm, D_MODEL), row)] * 2
    out_shape = [jax.ShapeDtypeStruct((m, D_MODEL), F32), jax.ShapeDtypeStruct((m, D_MODEL), BF16)]
    return pl.pallas_call(
        _outproj_ln_body,
        grid=(m // tm,),
        in_specs=[pl.BlockSpec((tm, o_fox.shape[1]), row),
                  pl.BlockSpec((tm, o_moba.shape[1]), row),
                  pl.BlockSpec((tm, o_diff.shape[1]), row),
                  pl.BlockSpec((D_MODEL, D_MODEL), fixed, pipeline_mode=pl.Buffered(1)),
                  pl.BlockSpec((tm, D_MODEL), row),
                  pl.BlockSpec((1, D_MODEL), fixed),
                  pl.BlockSpec((1, D_MODEL), fixed)],
        out_specs=out_specs,
        out_shape=out_shape,
        compiler_params=_cparams(1),
        name=name,
    )(o_fox, o_moba, o_diff, w_out, x32, g, b)


def _swiglu_up_body(x_ref, wg_ref, wu_ref, h_ref):
    x = x_ref[...]
    g = jnp.dot(x, wg_ref[...], preferred_element_type=F32)
    u = jnp.dot(x, wu_ref[...], preferred_element_type=F32)
    h_ref[...] = (g * (1.0 / (1.0 + jnp.exp(-g))) * u).astype(h_ref.dtype)


def _swiglu_up(xb, wg, wu, tm, tn, name):
    m, k = xb.shape
    f = wg.shape[1]
    tm, tn = min(tm, m), min(tn, f)
    return pl.pallas_call(
        _swiglu_up_body,
        grid=(m // tm, f // tn),
        in_specs=[pl.BlockSpec((tm, k), lambda i, j: (i, 0)),
                  pl.BlockSpec((k, tn), lambda i, j: (0, j)),
                  pl.BlockSpec((k, tn), lambda i, j: (0, j))],
        out_specs=pl.BlockSpec((tm, tn), lambda i, j: (i, j)),
        out_shape=jax.ShapeDtypeStruct((m, f), BF16),
        compiler_params=_cparams(2),
        name=name,
    )(xb, wg, wu)


def _down_ln_body(h_ref, w_ref, x_ref, g_ref, b_ref, o32_ref, ob_ref, acc):
    kk = pl.program_id(1)

    @pl.when(kk == 0)
    def _():
        acc[...] = jnp.zeros_like(acc)

    acc[...] += jnp.dot(h_ref[...], w_ref[...], preferred_element_type=F32)

    @pl.when(kk == pl.num_programs(1) - 1)
    def _():
        y = _layer_norm_rows(ALPHA * x_ref[...] + acc[...], g_ref[...], b_ref[...])
        o32_ref[...] = y
        ob_ref[...] = y.astype(BF16)


def _down_ln(h, wd, x32, g, b, tm, tk, name):
    m, f = h.shape
    tm, tk = min(tm, m), min(tk, f)
    row = lambda i, k: (i, 0)
    fixed = lambda i, k: (0, 0)
    w_mode = dict(pipeline_mode=pl.Buffered(1)) if tk == f else {}
    return pl.pallas_call(
        _down_ln_body,
        grid=(m // tm, f // tk),
        in_specs=[pl.BlockSpec((tm, tk), lambda i, k: (i, k)),
                  pl.BlockSpec((tk, D_MODEL), lambda i, k: (k, 0), **w_mode),
                  pl.BlockSpec((tm, D_MODEL), row),
                  pl.BlockSpec((1, D_MODEL), fixed),
                  pl.BlockSpec((1, D_MODEL), fixed)],
        out_specs=[pl.BlockSpec((tm, D_MODEL), row)] * 2,
        out_shape=[jax.ShapeDtypeStruct((m, D_MODEL), F32), jax.ShapeDtypeStruct((m, D_MODEL), BF16)],
        scratch_shapes=[pltpu.VMEM((tm, D_MODEL), F32)],
        compiler_params=_cparams(2),
        name=name,
    )(h, wd, x32, g, b)


def _online_softmax_step(s, v, m_sc, l_sc, acc_sc):
    m_prev = m_sc[...]
    m_new = jnp.maximum(m_prev, jnp.max(s, axis=1, keepdims=True))
    alpha = jnp.exp(m_prev - m_new)
    p = jnp.exp(s - jnp.tile(m_new, (1, s.shape[1] // LANES)))
    l_sc[...] = alpha * l_sc[...] + jnp.sum(p, axis=1, keepdims=True)
    acc_sc[...] = alpha * acc_sc[...] + jnp.dot(p.astype(BF16), v, preferred_element_type=F32)
    m_sc[...] = m_new


def _softmax_scratch(tq):
    return [pltpu.VMEM((tq, LANES), F32), pltpu.VMEM((tq, LANES), F32), pltpu.VMEM((tq, HEAD_DIM), F32)]


def _init_softmax_state(m_sc, l_sc, acc_sc):
    m_sc[...] = jnp.full(m_sc.shape, NEG_INF, F32)
    l_sc[...] = jnp.zeros_like(l_sc)
    acc_sc[...] = jnp.zeros_like(acc_sc)


def _qk(q, k):
    return lax.dot_general(q, k, (((1,), (1,)), ((), ())), preferred_element_type=F32)


def _lane_column(x, idx):
    lane = lax.broadcasted_iota(jnp.int32, x.shape, 1)
    return jnp.sum(jnp.where(lane == idx, x, 0.0), axis=1, keepdims=True)


def _fox_prompt_body(q_ref, k_ref, v_ref, frow_ref, fcol_ref, o_ref, m_sc, l_sc, acc_sc, *, tq):
    h = pl.program_id(1)
    qi = pl.program_id(2)
    scale = HEAD_DIM ** -0.5
    q = q_ref[...]
    frow = jnp.broadcast_to(_lane_column(frow_ref[...], h), (tq, LANES))
    _init_softmax_state(m_sc, l_sc, acc_sc)

    def step(c, diagonal):
        start = pl.multiple_of(c * tq, tq)
        s = _qk(q, k_ref[pl.ds(start, tq), :]) * scale
        s = s + (jnp.tile(frow, (1, tq // LANES)) - fcol_ref[0, 0, :, pl.ds(start, tq)])
        if diagonal:
            row = lax.broadcasted_iota(jnp.int32, (tq, tq), 0)
            col = lax.broadcasted_iota(jnp.int32, (tq, tq), 1)
            s = jnp.where(col <= row, s, NEG_INF)
        _online_softmax_step(s, v_ref[pl.ds(start, tq), :], m_sc, l_sc, acc_sc)

    step(qi, True)

    def past(c, carry):
        step(c, False)
        return carry

    lax.fori_loop(0, qi, past, 0)
    o_ref[...] = (acc_sc[...] / l_sc[...]).astype(o_ref.dtype)


def _fox_prompt(q_maps, kb, vb, f_rows, f_cols, n_seq, name):
    m = q_maps.shape[0]
    t_len = m // n_seq
    tq = min(ATT_TILE, t_len)
    nq = t_len // tq
    return pl.pallas_call(
        functools.partial(_fox_prompt_body, tq=tq),
        grid=(n_seq, H_FOX, nq),
        in_specs=[pl.BlockSpec((tq, HEAD_DIM), lambda b, h, i: (b * nq + i, h)),
                  pl.BlockSpec((t_len, HEAD_DIM), lambda b, h, i: (b, h)),
                  pl.BlockSpec((t_len, HEAD_DIM), lambda b, h, i: (b, h)),
                  pl.BlockSpec((tq, LANES), lambda b, h, i: (b * nq + i, 0)),
                  pl.BlockSpec((1, 1, 1, t_len), lambda b, h, i: (b, h, 0, 0))],
        out_specs=pl.BlockSpec((tq, HEAD_DIM), lambda b, h, i: (b * nq + i, h)),
        out_shape=jax.ShapeDtypeStruct((m, H_FOX * HEAD_DIM), BF16),
        scratch_shapes=_softmax_scratch(tq),
        compiler_params=_cparams(3),
        name=name,
    )(q_maps, kb, vb, f_rows, f_cols)


def _bias_tiles_body(rel_ref, o_ref, *, tq):
    h = pl.program_id(0)
    row = lax.broadcasted_iota(jnp.int32, (tq, tq), 0)
    col = lax.broadcasted_iota(jnp.int32, (tq, tq), 1)
    for kind in range(2):
        d = row - col + kind * tq
        bias = jnp.full((tq, tq), rel_ref[h, 0], F32)
        for bkt in range(1, N_BUCKETS):
            bias = jnp.where(d >= T5_THRESHOLDS[bkt], rel_ref[h, bkt], bias)
        if kind == 0:
            bias = jnp.where(d >= 0, bias, NEG_INF)
        o_ref[0, kind] = bias


def _bias_tiles(rel_t, tq):
    nh = rel_t.shape[0]
    return pl.pallas_call(
        functools.partial(_bias_tiles_body, tq=tq),
        grid=(nh,),
        in_specs=[pl.BlockSpec(memory_space=pltpu.SMEM)],
        out_specs=pl.BlockSpec((1, 2, tq, tq), lambda h: (h, 0, 0, 0)),
        out_shape=jax.ShapeDtypeStruct((nh, 2, tq, tq), F32),
        compiler_params=_cparams(1),
        name="t5_bias_tiles",
    )(rel_t)


def _block_means_body(k_ref, o_ref, *, n_blocks):
    k = k_ref[...]
    o_ref[...] = jnp.zeros_like(o_ref)
    km = jnp.mean(k.reshape(n_blocks, MOBA_BLOCK, HEAD_DIM), axis=1)
    o_ref[0, 0, 0:n_blocks, :] = km


def _block_means(k32, n_seq, name):
    m = k32.shape[0]
    t_len = m // n_seq
    n_blocks = t_len // MOBA_BLOCK
    return pl.pallas_call(
        functools.partial(_block_means_body, n_blocks=n_blocks),
        grid=(n_seq, H_MOBA),
        in_specs=[pl.BlockSpec((t_len, HEAD_DIM), lambda b, h: (b, H_FOX + h))],
        out_specs=pl.BlockSpec((1, 1, LANES, HEAD_DIM), lambda b, h: (b, h, 0, 0)),
        out_shape=jax.ShapeDtypeStruct((n_seq, H_MOBA, LANES, HEAD_DIM), F32),
        compiler_params=_cparams(2),
        name=name,
    )(k32)


def _top_blocks(gate, n_valid, n_candidates, n_sel):
    lane = lax.broadcasted_iota(jnp.int32, gate.shape, 1)
    rank = jnp.zeros(gate.shape, F32)
    for c2 in range(n_candidates):
        g2 = gate[:, c2:c2 + 1]
        beats = jnp.where(g2 > gate, 1.0, jnp.where(g2 == gate, jnp.where(lane > c2, 1.0, 0.0), 0.0))
        rank = rank + beats * jnp.where(c2 < n_valid, 1.0, 0.0)
    return jnp.where(rank < n_sel, jnp.where(lane < n_valid, 1.0, 0.0), 0.0)


def _moba_prompt_body(far_ref, q_ref, k_ref, v_ref, means_ref, bias_ref, o_ref, m_sc, l_sc, acc_sc, mask_sc,
                      *, tq, t_len, n_candidates, n_sel):
    h = pl.program_id(1)
    qi = pl.program_id(2)
    scale = HEAD_DIM ** -0.5
    q = q_ref[...]
    gate = lax.dot_general(q.astype(F32), means_ref[0, 0], (((1,), (1,)), ((), ())),
                           precision=HIGHEST, preferred_element_type=F32)
    row = lax.broadcasted_iota(jnp.int32, (tq, LANES), 0)
    lane = lax.broadcasted_iota(jnp.int32, (tq, LANES), 1)
    own = qi * (tq // MOBA_BLOCK) + lax.div(row, MOBA_BLOCK)
    sel = jnp.where(lane == own, 1.0, _top_blocks(gate, own, n_candidates, n_sel))
    for b in range(t_len // MOBA_BLOCK):
        slab = jnp.where(sel[:, b:b + 1] > 0.5, 0.0, NEG_INF)
        mask_sc[:, b * LANES:(b + 1) * LANES] = jnp.broadcast_to(slab, (tq, LANES))
    _init_softmax_state(m_sc, l_sc, acc_sc)
    blocks_per_chunk = tq // MOBA_BLOCK

    def step(c, bias):
        start = pl.multiple_of(c * tq, tq)
        slabs = mask_sc[:, pl.ds(pl.multiple_of(c * blocks_per_chunk * LANES, blocks_per_chunk * LANES),
                                 blocks_per_chunk * LANES)]
        mask = jnp.concatenate([jnp.tile(slabs[:, j * LANES:(j + 1) * LANES], (1, MOBA_BLOCK // LANES))
                                for j in range(blocks_per_chunk)], axis=1)
        s = _qk(q, k_ref[pl.ds(start, tq), :]) * scale + bias + mask
        _online_softmax_step(s, v_ref[pl.ds(start, tq), :], m_sc, l_sc, acc_sc)

    step(qi, bias_ref[0, 0])

    @pl.when(qi > 0)
    def _():
        step(qi - 1, bias_ref[0, 1])

    far = far_ref[h]

    def past(c, carry):
        step(c, far)
        return carry

    lax.fori_loop(0, jnp.maximum(qi - 1, 0), past, 0)
    o_ref[...] = (acc_sc[...] / l_sc[...]).astype(o_ref.dtype)


def _moba_prompt(q_maps, kb, vb, means, bias_tiles, far, n_seq, name):
    m = q_maps.shape[0]
    t_len = m // n_seq
    tq = min(ATT_TILE, t_len)
    nq = t_len // tq
    n_candidates = max((t_len - 1) // MOBA_BLOCK, 1)
    n_sel = min(MOBA_TOPK, n_candidates)
    grid_spec = pltpu.PrefetchScalarGridSpec(
        num_scalar_prefetch=1,
        grid=(n_seq, H_MOBA, nq),
        in_specs=[pl.BlockSpec((tq, HEAD_DIM), lambda b, h, i, far: (b * nq + i, H_FOX + h)),
                  pl.BlockSpec((t_len, HEAD_DIM), lambda b, h, i, far: (b, H_FOX + h)),
                  pl.BlockSpec((t_len, HEAD_DIM), lambda b, h, i, far: (b, H_FOX + h)),
                  pl.BlockSpec((1, 1, LANES, HEAD_DIM), lambda b, h, i, far: (b, h, 0, 0)),
                  pl.BlockSpec((1, 2, tq, tq), lambda b, h, i, far: (h, 0, 0, 0))],
        out_specs=pl.BlockSpec((tq, HEAD_DIM), lambda b, h, i, far: (b * nq + i, h)),
        scratch_shapes=_softmax_scratch(tq) + [pltpu.VMEM((tq, (t_len // MOBA_BLOCK) * LANES), F32)],
    )
    return pl.pallas_call(
        functools.partial(_moba_prompt_body, tq=tq, t_len=t_len, n_candidates=n_candidates, n_sel=n_sel),
        grid_spec=grid_spec,
        out_shape=jax.ShapeDtypeStruct((m, H_MOBA * HEAD_DIM), BF16),
        compiler_params=_cparams(3),
        name=name,
    )(far, q_maps, kb, vb, means, bias_tiles)


def _diff_lambda(lam_ref, lam_init):
    r = lam_ref[...]
    s1 = jnp.sum(r[0:1, :] * r[1:2, :], axis=1, keepdims=True)
    s2 = jnp.sum(r[2:3, :] * r[3:4, :], axis=1, keepdims=True)
    return jnp.exp(s1) - jnp.exp(s2) + lam_init


def _diff_finish(o0, o1, lam, g, lam_init):
    o = o0 - lam * o1
    return o * lax.rsqrt(jnp.mean(o * o, axis=1, keepdims=True) + LN_EPS) * g * (1.0 - lam_init)


def _diff_prompt_body(far_ref, q_ref, k_ref, v_ref, bias_ref, lam_ref, g_ref, o_ref,
                      m0, l0, a0, m1, l1, a1, *, tq, lam_init):
    h = pl.program_id(1)
    qi = pl.program_id(2)
    scale = DIFF_QK_DIM ** -0.5
    q0 = q_ref[:, 0:HEAD_DIM]
    q1 = q_ref[:, HEAD_DIM:2 * HEAD_DIM]
    _init_softmax_state(m0, l0, a0)
    _init_softmax_state(m1, l1, a1)

    def step(c, bias):
        start = pl.multiple_of(c * tq, tq)
        k = k_ref[pl.ds(start, tq), :]
        v = v_ref[pl.ds(start, tq), :]
        _online_softmax_step(_qk(q0, k) * scale + bias, v, m0, l0, a0)
        _online_softmax_step(_qk(q1, k) * scale + bias, v, m1, l1, a1)

    step(qi, bias_ref[0, 0])

    @pl.when(qi > 0)
    def _():
        step(qi - 1, bias_ref[0, 1])

    far = far_ref[H_MOBA + h]

    def past(c, carry):
        step(c, far)
        return carry

    lax.fori_loop(0, jnp.maximum(qi - 1, 0), past, 0)
    lam = _diff_lambda(lam_ref, lam_init)
    o = _diff_finish(a0[...] / l0[...], a1[...] / l1[...], lam, g_ref[...], lam_init)
    o_ref[...] = o.astype(o_ref.dtype)


def _diff_prompt(q_maps, kb, vb, bias_tiles, far, lam_rows, g, lam_init, n_seq, name):
    m = q_maps.shape[0]
    t_len = m // n_seq
    tq = min(ATT_TILE, t_len)
    nq = t_len // tq
    h0 = H_FOX + H_MOBA
    grid_spec = pltpu.PrefetchScalarGridSpec(
        num_scalar_prefetch=1,
        grid=(n_seq, H_DIFF, nq),
        in_specs=[pl.BlockSpec((tq, 2 * HEAD_DIM), lambda b, h, i, far: (b * nq + i, h0 // 2 + h)),
                  pl.BlockSpec((t_len, HEAD_DIM), lambda b, h, i, far: (b, h0 + h)),
                  pl.BlockSpec((t_len, HEAD_DIM), lambda b, h, i, far: (b, h0 + h)),
                  pl.BlockSpec((1, 2, tq, tq), lambda b, h, i, far: (H_MOBA + h, 0, 0, 0)),
                  pl.BlockSpec((8, LANES), lambda b, h, i, far: (0, 0)),
                  pl.BlockSpec((1, HEAD_DIM), lambda b, h, i, far: (0, 0))],
        out_specs=pl.BlockSpec((tq, HEAD_DIM), lambda b, h, i, far: (b * nq + i, h)),
        scratch_shapes=_softmax_scratch(tq) + _softmax_scratch(tq),
    )
    return pl.pallas_call(
        functools.partial(_diff_prompt_body, tq=tq, lam_init=lam_init),
        grid_spec=grid_spec,
        out_shape=jax.ShapeDtypeStruct((m, H_DIFF * HEAD_DIM), BF16),
        compiler_params=_cparams(3),
        name=name,
    )(far, q_maps, kb, vb, bias_tiles, lam_rows, g)


def _sample_page_body(pt_ref, q_ref, *refs):
    g = SAMPLE_PAGES_PER_STEP
    k_refs, v_refs, sfx_refs = refs[0:g], refs[g:2 * g], refs[2 * g:3 * g]
    cb_ref, m_ref, l_ref, acc_ref, ksum_ref = refs[3 * g:]
    n_cols = PAGE_SIZE * N_HEADS
    row = lax.broadcasted_iota(jnp.int32, (SAMPLE_ROWS, n_cols), 0)
    col = lax.broadcasted_iota(jnp.int32, (SAMPLE_ROWS, n_cols), 1)
    scale = jnp.where(row < H_FOX + H_MOBA, HEAD_DIM ** -0.5, DIFF_QK_DIM ** -0.5).astype(F32)
    same_head = (col & (N_HEADS - 1)) == (row & (N_HEADS - 1))
    q = q_ref[0]
    for j in range(g):
        kp = k_refs[j][0, 0]
        kf = kp.reshape(n_cols, HEAD_DIM).astype(BF16)
        vf = v_refs[j][0, 0].reshape(n_cols, HEAD_DIM).astype(BF16)
        s = _qk(q, kf) * scale + (sfx_refs[j][0, 0] + cb_ref[0, j])
        s = jnp.where(same_head, s, NEG_INF)
        m = jnp.max(s, axis=1, keepdims=True)
        p = jnp.exp(s - m)
        l = jnp.sum(p, axis=1, keepdims=True)
        acc_ref[0, j] = jnp.dot(p.astype(BF16), vf, preferred_element_type=F32)
        m_ref[0, j] = jnp.broadcast_to(m, (SAMPLE_ROWS, LANES))
        l_ref[0, j] = jnp.broadcast_to(l, (SAMPLE_ROWS, LANES))
        ksum_ref[0, j] = jnp.sum(kp, axis=0)


def _sample_pages(layer, page_table, q_rows, cache_k, cache_v, sfx_rows, const_rows, name):
    n_seq, n_pages = page_table.shape
    g = SAMPLE_PAGES_PER_STEP
    n_cols = PAGE_SIZE * N_HEADS
    page_block = (1, 1, PAGE_SIZE, N_HEADS, HEAD_DIM)

    def page_spec(j):
        return pl.BlockSpec(page_block, lambda b, p, pt: (layer, pt[b, p * g + j], 0, 0, 0))

    def sfx_spec(j):
        return pl.BlockSpec((1, 1, 1, n_cols), lambda b, p, pt: (0, pt[b, p * g + j], 0, 0))

    stat_spec = pl.BlockSpec((1, g, SAMPLE_ROWS, LANES), lambda b, p, pt: (b, p, 0, 0))
    grid_spec = pltpu.PrefetchScalarGridSpec(
        num_scalar_prefetch=1,
        grid=(n_seq, n_pages // g),
        in_specs=([pl.BlockSpec((1, SAMPLE_ROWS, HEAD_DIM), lambda b, p, pt: (b, 0, 0))]
                  + [page_spec(j) for j in range(g)] + [page_spec(j) for j in range(g)]
                  + [sfx_spec(j) for j in range(g)]
                  + [pl.BlockSpec((1, g, 1, n_cols), lambda b, p, pt: (b, p, 0, 0))]),
        out_specs=[stat_spec, stat_spec, stat_spec,
                   pl.BlockSpec((1, g, N_HEADS, HEAD_DIM), lambda b, p, pt: (b, p, 0, 0))],
    )
    stat_shape = jax.ShapeDtypeStruct((n_seq, n_pages, SAMPLE_ROWS, LANES), F32)
    return pl.pallas_call(
        _sample_page_body,
        grid_spec=grid_spec,
        out_shape=[stat_shape, stat_shape, stat_shape,
                   jax.ShapeDtypeStruct((n_seq, n_pages, N_HEADS, HEAD_DIM), F32)],
        compiler_params=_cparams(2),
        name=name,
    )(page_table, q_rows, *([cache_k] * g), *([cache_v] * g), *([sfx_rows] * g), const_rows)


def _sample_combine_body(m_ref, l_ref, acc_ref, ksum_ref, q_ref, k_ref, v_ref, sb_ref, lam_ref, g_ref, o_ref,
                         *, n_pages, lam_init):
    n_blocks = n_pages * PAGE_SIZE // MOBA_BLOCK
    pages_per_block = MOBA_BLOCK // PAGE_SIZE
    n_sel = min(MOBA_TOPK, n_blocks)
    m = m_ref[0]
    l = l_ref[0]
    acc = acc_ref[0]
    q = q_ref[0]
    head = lax.broadcasted_iota(jnp.int32, (N_HEADS, LANES), 0)
    is_moba = jnp.where(head >= H_FOX, jnp.where(head < H_FOX + H_MOBA, 1.0, 0.0), 0.0)

    gp = jnp.sum(ksum_ref[0] * q[0:N_HEADS][None], axis=2, keepdims=True)
    gb = jnp.sum(gp.reshape(n_blocks, pages_per_block, N_HEADS, 1), axis=1) * (1.0 / MOBA_BLOCK)
    gb = jnp.broadcast_to(gb, (n_blocks, N_HEADS, LANES))
    blk = lax.broadcasted_iota(jnp.int32, (n_blocks, N_HEADS, LANES), 0)
    rank = jnp.zeros((n_blocks, N_HEADS, LANES), F32)
    for c2 in range(n_blocks):
        g2 = gb[c2:c2 + 1]
        rank = rank + jnp.where(g2 > gb, 1.0, jnp.where(g2 == gb, jnp.where(blk > c2, 1.0, 0.0), 0.0))
    sel_b = jnp.where(rank < n_sel, 1.0, 0.0)
    sel_b = jnp.maximum(sel_b, 1.0 - is_moba[None])
    sel_p = jnp.broadcast_to(sel_b[:, None], (n_blocks, pages_per_block, N_HEADS, LANES))
    sel_p = sel_p.reshape(n_pages, N_HEADS, LANES)
    sel = jnp.concatenate([sel_p, jnp.ones_like(sel_p)], axis=1)

    row = lax.broadcasted_iota(jnp.int32, (SAMPLE_ROWS, 1), 0)
    scale = jnp.where(row < H_FOX + H_MOBA, HEAD_DIM ** -0.5, DIFF_QK_DIM ** -0.5).astype(F32)
    s_self = jnp.sum(q * k_ref[0], axis=1, keepdims=True) * scale + sb_ref[0]

    m_sel = jnp.where(sel > 0.0, m, NEG_INF)
    m_all = jnp.maximum(jnp.max(m_sel, axis=0), s_self)
    w = jnp.where(sel > 0.0, jnp.exp(m_sel - m_all[None]), 0.0)
    w_self = jnp.exp(s_self - m_all)
    den = jnp.sum(w * l, axis=0) + w_self
    num = jnp.sum(w * acc, axis=0) + w_self * v_ref[0]
    o = num / den
    o_first, o_second = o[0:N_HEADS], o[N_HEADS:SAMPLE_ROWS]
    lam = _diff_lambda(lam_ref, lam_init)
    od = _diff_finish(o_first, o_second, lam, g_ref[...], lam_init)
    o_ref[0] = jnp.where(head >= H_FOX + H_MOBA, od, o_first).astype(o_ref.dtype)


def _sample_combine(m, l, acc, ksum, q_rows32, k_rows, v_rows, self_bias, lam_rows, g, lam_init, name):
    n_seq, n_pages = m.shape[:2]
    stat_spec = pl.BlockSpec((1, n_pages, SAMPLE_ROWS, LANES), lambda b: (b, 0, 0, 0))
    row_spec = pl.BlockSpec((1, SAMPLE_ROWS, HEAD_DIM), lambda b: (b, 0, 0))
    return pl.pallas_call(
        functools.partial(_sample_combine_body, n_pages=n_pages, lam_init=lam_init),
        grid=(n_seq,),
        in_specs=[stat_spec, stat_spec, stat_spec,
                  pl.BlockSpec((1, n_pages, N_HEADS, HEAD_DIM), lambda b: (b, 0, 0, 0)),
                  row_spec, row_spec, row_spec,
                  pl.BlockSpec((1, SAMPLE_ROWS, LANES), lambda b: (0, 0, 0)),
                  pl.BlockSpec((8, LANES), lambda b: (0, 0)),
                  pl.BlockSpec((1, HEAD_DIM), lambda b: (0, 0))],
        out_specs=pl.BlockSpec((1, N_HEADS, HEAD_DIM), lambda b: (b, 0, 0)),
        out_shape=jax.ShapeDtypeStruct((n_seq, N_HEADS, HEAD_DIM), BF16),
        compiler_params=_cparams(1),
        name=name,
    )(m, l, acc, ksum, q_rows32, k_rows, v_rows, self_bias, lam_rows, g)


def _router_body(x_ref, w_ref, b_ref, o_ref):
    logits = jnp.dot(x_ref[...], w_ref[...], precision=HIGHEST, preferred_element_type=F32) + b_ref[...]
    lane = lax.broadcasted_iota(jnp.int32, logits.shape, 1)
    lane_f = lane.astype(F32)
    logits = jnp.where(lane < N_EXPERTS, logits, NEG_INF)
    v1 = jnp.max(logits, axis=1, keepdims=True)
    i1 = jnp.min(jnp.where(logits == v1, lane_f, float(LANES)), axis=1, keepdims=True)
    rest = jnp.where(lane_f == i1, NEG_INF, logits)
    v2 = jnp.max(rest, axis=1, keepdims=True)
    i2 = jnp.min(jnp.where(rest == v2, lane_f, float(LANES)), axis=1, keepdims=True)
    e2 = jnp.exp(v2 - v1)
    g1 = 1.0 / (1.0 + e2)
    g2 = e2 / (1.0 + e2)
    out = jnp.where(lane == 0, i1, jnp.where(lane == 1, i2, jnp.where(lane == 2, g1, jnp.where(lane == 3, g2, 0.0))))
    o_ref[...] = out


def _router(x32, w_r, b_r, tm, name):
    m = x32.shape[0]
    tm = min(tm, m)
    return pl.pallas_call(
        _router_body,
        grid=(m // tm,),
        in_specs=[pl.BlockSpec((tm, D_MODEL), lambda i: (i, 0)),
                  pl.BlockSpec((D_MODEL, LANES), lambda i: (0, 0)),
                  pl.BlockSpec((1, LANES), lambda i: (0, 0))],
        out_specs=pl.BlockSpec((tm, LANES), lambda i: (i, 0)),
        out_shape=jax.ShapeDtypeStruct((m, LANES), F32),
        compiler_params=_cparams(1),
        name=name,
    )(x32, w_r, b_r)


def _row_copy(src_hbm, idx, dst, r, sem):
    return pltpu.make_async_copy(src_hbm.at[pl.ds(idx, 1), :], dst.at[pl.ds(r, 1), :], sem)


def _start_row_gather(src_hbm, idx_ref, dst, sem, n_rows):
    def start(r0, carry):
        for u in range(GATHER_UNROLL):
            r = r0 * GATHER_UNROLL + u
            _row_copy(src_hbm, idx_ref[0, 0, r], dst, r, sem).start(priority=u % 2)
        return carry

    lax.fori_loop(0, n_rows // GATHER_UNROLL, start, 0)


def _wait_row_gather(src_hbm, dst, sem, n_rows):
    pltpu.make_async_copy(src_hbm.at[pl.ds(0, n_rows), :], dst, sem).wait()


def _expert_up_body(te_ref, nu_ref, idx_ref, idx_next_ref, x_hbm, wg_ref, wu_ref, h_ref, rows, xb, sems, *, tm):
    i = pl.program_id(0)
    n_used = nu_ref[0]
    slot = lax.rem(i, 2)

    @pl.when(jnp.logical_and(pl.program_id(1) == 0, i < n_used))
    def _():
        @pl.when(i == 0)
        def _():
            _start_row_gather(x_hbm, idx_ref, rows.at[0], sems.at[0], tm)

        _wait_row_gather(x_hbm, rows.at[slot], sems.at[slot], tm)

        @pl.when(i + 1 < n_used)
        def _():
            _start_row_gather(x_hbm, idx_next_ref, rows.at[1 - slot], sems.at[1 - slot], tm)

        xb[...] = rows[slot].astype(BF16)

    @pl.when(i < n_used)
    def _():
        x = xb[...]
        g = jnp.dot(x, wg_ref[0], preferred_element_type=F32)
        u = jnp.dot(x, wu_ref[0], preferred_element_type=F32)
        h_ref[...] = (g * (1.0 / (1.0 + jnp.exp(-g))) * u).astype(h_ref.dtype)

    @pl.when(i >= n_used)
    def _():
        h_ref[...] = jnp.zeros_like(h_ref)


def _expert_up(x32, row_token, wg, wu, tile_expert, n_used, tm, tn, name):
    p = row_token.shape[0]
    k = x32.shape[1]
    f = wg.shape[2]
    tn = min(tn, f)
    nf = f // tn
    nt = p // tm
    idx = row_token.reshape(nt, 1, tm)
    wspec = pl.BlockSpec((1, k, tn), lambda i, j, te, nu: (te[i], 0, jnp.where(i < nu[0], j, nf - 1)))
    grid_spec = pltpu.PrefetchScalarGridSpec(
        num_scalar_prefetch=2,
        grid=(nt, nf),
        in_specs=[pl.BlockSpec((1, 1, tm), lambda i, j, te, nu: (i, 0, 0), memory_space=pltpu.SMEM),
                  pl.BlockSpec((1, 1, tm), lambda i, j, te, nu: (jnp.minimum(i + 1, nt - 1), 0, 0),
                               memory_space=pltpu.SMEM),
                  pl.BlockSpec(memory_space=pl.ANY), wspec, wspec],
        out_specs=pl.BlockSpec((tm, tn), lambda i, j, te, nu: (i, j)),
        scratch_shapes=[pltpu.VMEM((2, tm, k), F32), pltpu.VMEM((tm, k), BF16), pltpu.SemaphoreType.DMA((2,))],
    )
    return pl.pallas_call(
        functools.partial(_expert_up_body, tm=tm),
        grid_spec=grid_spec,
        out_shape=jax.ShapeDtypeStruct((p, f), BF16),
        compiler_params=_cparams(2),
        name=name,
    )(tile_expert, n_used, idx, idx, x32, wg, wu)


def _expert_down_body(te_ref, nu_ref, h_ref, w_ref, y_ref, acc):
    kk = pl.program_id(1)

    @pl.when(pl.program_id(0) < nu_ref[0])
    def _():
        @pl.when(kk == 0)
        def _():
            acc[...] = jnp.zeros_like(acc)

        acc[...] += jnp.dot(h_ref[...], w_ref[0], preferred_element_type=F32)

        @pl.when(kk == pl.num_programs(1) - 1)
        def _():
            y_ref[...] = acc[...]

    @pl.when(pl.program_id(0) >= nu_ref[0])
    def _():
        y_ref[...] = jnp.zeros_like(y_ref)


def _expert_down(h, wd, tile_expert, n_used, tm, tk, name):
    p, f = h.shape
    tk = min(tk, f)
    nk = f // tk
    grid_spec = pltpu.PrefetchScalarGridSpec(
        num_scalar_prefetch=2,
        grid=(p // tm, nk),
        in_specs=[pl.BlockSpec((tm, tk), lambda i, k, te, nu: (i, k)),
                  pl.BlockSpec((1, tk, D_MODEL), lambda i, k, te, nu: (te[i], jnp.where(i < nu[0], k, nk - 1), 0))],
        out_specs=pl.BlockSpec((tm, D_MODEL), lambda i, k, te, nu: (i, 0)),
        scratch_shapes=[pltpu.VMEM((tm, D_MODEL), F32)],
    )
    return pl.pallas_call(
        _expert_down_body,
        grid_spec=grid_spec,
        out_shape=jax.ShapeDtypeStruct((p, D_MODEL), F32),
        compiler_params=_cparams(2),
        name=name,
    )(tile_expert, n_used, h, wd)


def _moe_combine_body(p0_ref, p1_ref, y_hbm, r_ref, x_ref, g_ref, b_ref, o32_ref, ob_ref, buf0, buf1, sem0, sem1,
                      *, tr):
    _start_row_gather(y_hbm, p0_ref, buf0, sem0, tr)
    _start_row_gather(y_hbm, p1_ref, buf1, sem1, tr)
    _wait_row_gather(y_hbm, buf0, sem0, tr)
    _wait_row_gather(y_hbm, buf1, sem1, tr)
    r = r_ref[...]
    y = ALPHA * x_ref[...] + (r[:, 2:3] * buf0[...] + r[:, 3:4] * buf1[...])
    y = _layer_norm_rows(y, g_ref[...], b_ref[...])
    o32_ref[...] = y
    ob_ref[...] = y.astype(BF16)


def _moe_combine(y_sorted, pos0, pos1, routes, x32, g, b, tr, name):
    m = x32.shape[0]
    tr = min(tr, m)
    idx_spec = pl.BlockSpec((1, 1, tr), lambda i: (i, 0, 0), memory_space=pltpu.SMEM)
    row = lambda i: (i, 0)
    fixed = lambda i: (0, 0)
    return pl.pallas_call(
        functools.partial(_moe_combine_body, tr=tr),
        grid=(m // tr,),
        in_specs=[idx_spec, idx_spec,
                  pl.BlockSpec(memory_space=pl.ANY),
                  pl.BlockSpec((tr, LANES), row),
                  pl.BlockSpec((tr, D_MODEL), row),
                  pl.BlockSpec((1, D_MODEL), fixed),
                  pl.BlockSpec((1, D_MODEL), fixed)],
        out_specs=[pl.BlockSpec((tr, D_MODEL), row)] * 2,
        out_shape=[jax.ShapeDtypeStruct((m, D_MODEL), F32), jax.ShapeDtypeStruct((m, D_MODEL), BF16)],
        scratch_shapes=[pltpu.VMEM((tr, D_MODEL), F32), pltpu.VMEM((tr, D_MODEL), F32),
                        pltpu.SemaphoreType.DMA(()), pltpu.SemaphoreType.DMA(())],
        compiler_params=_cparams(1),
        name=name,
    )(pos0.reshape(m // tr, 1, tr), pos1.reshape(m // tr, 1, tr), y_sorted, routes, x32, g, b)


def _routing_plan(routes, tm):
    m = routes.shape[0]
    experts = jnp.concatenate([routes[:, 0], routes[:, 1]]).astype(jnp.int32)
    onehot = (experts[:, None] == jnp.arange(N_EXPERTS, dtype=jnp.int32)[None, :]).astype(jnp.int32)
    csum = jnp.cumsum(onehot, axis=0)
    rank = jnp.sum(csum * onehot, axis=1) - 1
    counts = csum[-1]
    padded = ((counts + tm - 1) // tm) * tm
    ends = jnp.cumsum(padded)
    starts = ends - padded
    pos = jnp.sum(onehot * starts[None, :], axis=1) + rank
    n_rows = (pl.cdiv(TOP_K * m, tm) + N_EXPERTS) * tm
    token = jnp.concatenate([jnp.arange(m, dtype=jnp.int32)] * TOP_K)
    row_token = jnp.zeros((n_rows,), jnp.int32).at[pos].set(token)
    tile_start = jnp.arange(n_rows // tm, dtype=jnp.int32) * tm
    tile_expert = jnp.minimum(jnp.sum((tile_start[:, None] >= ends[None, :]).astype(jnp.int32), axis=1),
                              N_EXPERTS - 1)
    n_used = (ends[-1] // tm).reshape(1).astype(jnp.int32)
    return row_token, pos[:m], pos[m:], tile_expert, n_used


def _moe_layer(x32, w_r, b_r, wg, wu, wd, g, b, tm, tn, tk, tag):
    routes = _router(x32, w_r, b_r, 512, f"router_{tag}")
    row_token, pos0, pos1, tile_expert, n_used = _routing_plan(routes, tm)
    h = _expert_up(x32, row_token, wg, wu, tile_expert, n_used, tm, tn, f"moe_up_{tag}")
    y = _expert_down(h, wd, tile_expert, n_used, tm, tk, f"moe_down_{tag}")
    return _moe_combine(y, pos0, pos1, routes, x32, g, b, 256, f"moe_combine_{tag}")


def _q_map_weights(w_q):
    h0 = (H_FOX + H_MOBA) * HEAD_DIM
    cols = [w_q[:, :h0]]
    zeros = jnp.zeros((w_q.shape[0], DIFF_QK_DIM), w_q.dtype)
    for i in range(H_DIFF):
        wh = w_q[:, h0 + i * HEAD_DIM:h0 + (i + 1) * HEAD_DIM]
        cols += [wh[:, :DIFF_QK_DIM], zeros, zeros, wh[:, DIFF_QK_DIM:]]
    return jnp.concatenate(cols, axis=1)


def _sample_row_order():
    first = list(range(H_FOX + H_MOBA)) + [H_FOX + H_MOBA + 2 * i for i in range(H_DIFF)]
    second = [H_FOX + H_MOBA + 2 * i + 1 for i in range(H_DIFF)]
    return first, second


def kernel(x_prompt, x_sample, cache_k, cache_v, cache_logf, page_table, w_in, b_forget, w_out, lam_q1, lam_k1, lam_q2, lam_k2, subln_g, rel_bias, ln1_g, ln1_b, ln2_g, ln2_b, w_ffn_gate, w_ffn_up, w_ffn_down, w_router, b_router, w_exp_gate, w_exp_up, w_exp_down):
    n_seq, t_len, _ = x_prompt.shape
    n_dec, dec_seq, _ = x_sample.shape
    assert dec_seq == 1 and t_len % MOBA_BLOCK == 0
    n_pages = page_table.shape[1]
    past = n_pages * PAGE_SIZE
    assert past % MOBA_BLOCK == 0 and past >= MOBA_BLOCK and n_pages % SAMPLE_PAGES_PER_STEP == 0
    depth = w_in.shape[0]
    n_pool = cache_k.shape[1]
    mp, ms = n_seq * t_len, n_dec
    n_cols = PAGE_SIZE * N_HEADS

    rel_t = jnp.transpose(rel_bias).astype(F32)
    far = rel_t[:, N_BUCKETS - 1]
    bias_tiles = _bias_tiles(rel_t, min(ATT_TILE, t_len))
    dist = past - (jnp.arange(n_pages)[:, None] * PAGE_SIZE + jnp.arange(PAGE_SIZE)[None, :])
    bucket_of = jnp.array([_t5_bucket_static(d) for d in range(T5_FAR + 1)], jnp.int32)
    bkt = bucket_of[jnp.minimum(dist, T5_FAR)]
    t5_cols = jnp.concatenate([jnp.zeros((n_pages, PAGE_SIZE, H_FOX), F32), rel_bias[bkt]], axis=2)
    first_rows, second_rows = _sample_row_order()
    self_bias16 = jnp.concatenate([jnp.zeros((H_FOX,), F32), rel_bias[0]])
    self_bias = jnp.broadcast_to(jnp.concatenate([self_bias16, self_bias16])[None, :, None],
                                 (1, SAMPLE_ROWS, LANES))

    xp32 = x_prompt.reshape(mp, D_MODEL)
    xs32 = x_sample.reshape(ms, D_MODEL)
    xpb, xsb = xp32.astype(BF16), xs32.astype(BF16)
    outs = {k: [] for k in ("kp", "vp", "fp", "ks", "vs", "fs")}

    for l in range(depth):
        lam_init = 0.8 - 0.6 * math.exp(-0.3 * l)
        w_q = _q_map_weights(w_in[l, :, :D_MODEL]).astype(BF16)
        w_k = w_in[l, :, D_MODEL:2 * D_MODEL].astype(BF16)
        w_v = w_in[l, :, 2 * D_MODEL:3 * D_MODEL].astype(BF16)
        w_f32 = jnp.pad(w_in[l, :, 3 * D_MODEL:], ((0, 0), (0, LANES - H_FOX)))
        w_f_hi = w_f32.astype(BF16)
        w_f = jnp.concatenate([w_f_hi, (w_f32 - w_f_hi.astype(F32)).astype(BF16)], axis=1)
        b_f = jnp.pad(b_forget[l], (0, LANES - H_FOX)).reshape(1, LANES)
        w_o = w_out[l].astype(BF16)
        lam_rows = jnp.pad(jnp.stack([lam_q1[l], lam_k1[l], lam_q2[l], lam_k2[l]]),
                           ((0, 4), (0, LANES - DIFF_QK_DIM)))
        g_sub = subln_g[l].reshape(1, HEAD_DIM)
        g1, b1 = ln1_g[l].reshape(1, D_MODEL), ln1_b[l].reshape(1, D_MODEL)
        g2, b2 = ln2_g[l].reshape(1, D_MODEL), ln2_b[l].reshape(1, D_MODEL)

        (qp,) = _matmul(xpb, w_q, [BF16], 1024, 1280, f"proj_q_p{l}")
        kp32, kpb = _matmul(xpb, w_k, [F32, BF16], 1024, 1024, f"proj_k_p{l}")
        vp32, vpb = _matmul(xpb, w_v, [F32, BF16], 1024, 1024, f"proj_v_p{l}")
        lfp, fcum = _log_forget(xp32, w_f, b_f, n_seq, True, f"logf_p{l}")
        f_cols = jnp.transpose(fcum.reshape(n_seq, t_len, LANES)[:, :, :8], (0, 2, 1)).reshape(n_seq, 8, 1, t_len)
        o_fox = _fox_prompt(qp, kpb, vpb, fcum, f_cols, n_seq, f"fox_p{l}")
        means = _block_means(kp32, n_seq, f"moba_means_p{l}")
        o_moba = _moba_prompt(qp, kpb, vpb, means, bias_tiles, far, n_seq, f"moba_p{l}")
        o_diff = _diff_prompt(qp, kpb, vpb, bias_tiles, far, lam_rows, g_sub, lam_init, n_seq, f"diff_p{l}")
        xp32, xpb = _outproj_ln(o_fox, o_moba, o_diff, w_o, xp32, g1, b1, 256, f"outproj_p{l}")

        (qs,) = _matmul(xsb, w_q, [F32], 512, 512, f"proj_q_s{l}")
        ks32, _ = _matmul(xsb, w_k, [F32, BF16], 512, 512, f"proj_k_s{l}")
        vs32, _ = _matmul(xsb, w_v, [F32, BF16], 512, 512, f"proj_v_s{l}")
        lfs, _ = _log_forget(xs32, w_f, b_f, n_dec, False, f"logf_s{l}")
        qs = qs.reshape(ms, N_QMAPS, HEAD_DIM)
        pad_rows = jnp.zeros((ms, SAMPLE_ROWS - N_HEADS - H_DIFF, HEAD_DIM), F32)
        q_rows32 = jnp.concatenate([qs[:, jnp.array(first_rows)], pad_rows, qs[:, jnp.array(second_rows)]], axis=1)
        k_heads = ks32.reshape(ms, N_HEADS, HEAD_DIM)
        v_heads = vs32.reshape(ms, N_HEADS, HEAD_DIM)
        k_rows = jnp.concatenate([k_heads, k_heads], axis=1)
        v_rows = jnp.concatenate([v_heads, v_heads], axis=1)
        lf_pool = cache_logf[l].astype(F32)
        pool_tot = jnp.sum(lf_pool, axis=1)
        sfx_pool = pool_tot[:, None, :] - jnp.cumsum(lf_pool, axis=1)
        sfx_rows = jnp.pad(sfx_pool, ((0, 0), (0, 0), (0, N_HEADS - H_FOX))).reshape(1, n_pool, 1, n_cols)
        page_tot = pool_tot[page_table]
        later = jnp.sum(page_tot, axis=1, keepdims=True) - jnp.cumsum(page_tot, axis=1)
        fox_const = later + lfs[:, None, :H_FOX]
        const16 = jnp.concatenate(
            [jnp.broadcast_to(fox_const[:, :, None, :], (ms, n_pages, PAGE_SIZE, H_FOX)),
             jnp.broadcast_to(t5_cols[None, :, :, H_FOX:], (ms, n_pages, PAGE_SIZE, N_HEADS - H_FOX))], axis=3)
        const_rows = const16.reshape(ms, n_pages, 1, n_cols)
        m_s, l_s, acc_s, ksum = _sample_pages(l, page_table, q_rows32.astype(BF16), cache_k, cache_v,
                                              sfx_rows, const_rows, f"sample_pages{l}")
        o_s = _sample_combine(m_s, l_s, acc_s, ksum, q_rows32, k_rows, v_rows, self_bias, lam_rows, g_sub,
                              lam_init, f"sample_combine{l}").reshape(ms, D_MODEL)
        nf, nm = H_FOX * HEAD_DIM, (H_FOX + H_MOBA) * HEAD_DIM
        xs32, xsb = _outproj_ln(o_s[:, :nf], o_s[:, nf:nm], o_s[:, nm:], w_o, xs32, g1, b1, 256, f"outproj_s{l}")

        i = l // 2
        if l % 2 == 0:
            wg, wu, wd = (w_ffn_gate[i].astype(BF16), w_ffn_up[i].astype(BF16), w_ffn_down[i].astype(BF16))
            hp = _swiglu_up(xpb, wg, wu, 1024, 512, f"ffn_up_p{l}")
            xp32, xpb = _down_ln(hp, wd, xp32, g2, b2, 256, wd.shape[0], f"ffn_down_p{l}")
            hs = _swiglu_up(xsb, wg, wu, 1024, 512, f"ffn_up_s{l}")
            xs32, xsb = _down_ln(hs, wd, xs32, g2, b2, 512, 1408, f"ffn_down_s{l}")
        else:
            wg, wu, wd = (w_exp_gate[i].astype(BF16), w_exp_up[i].astype(BF16), w_exp_down[i].astype(BF16))
            w_r = jnp.pad(w_router[i], ((0, 0), (0, LANES - N_EXPERTS)))
            b_r = jnp.pad(b_router[i], (0, LANES - N_EXPERTS)).reshape(1, LANES)
            xp32, xpb = _moe_layer(xp32, w_r, b_r, wg, wu, wd, g2, b2, 512, 512, 2816, f"p{l}")
            xs32, xsb = _moe_layer(xs32, w_r, b_r, wg, wu, wd, g2, b2, 16, 512, 1408, f"s{l}")

        outs["kp"].append(kp32.reshape(n_seq, t_len // PAGE_SIZE, PAGE_SIZE, N_HEADS, HEAD_DIM))
        outs["vp"].append(vp32.reshape(n_seq, t_len // PAGE_SIZE, PAGE_SIZE, N_HEADS, HEAD_DIM))
        outs["fp"].append(lfp[:, :H_FOX].reshape(n_seq, t_len // PAGE_SIZE, PAGE_SIZE, H_FOX))
        outs["ks"].append(ks32.reshape(n_dec, 1, N_HEADS, HEAD_DIM))
        outs["vs"].append(vs32.reshape(n_dec, 1, N_HEADS, HEAD_DIM))
        outs["fs"].append(lfs[:, :H_FOX].reshape(n_dec, 1, H_FOX))

    return (xp32.reshape(n_seq, t_len, D_MODEL), xs32.reshape(n_dec, 1, D_MODEL),
            jnp.stack(outs["kp"]), jnp.stack(outs["vp"]), jnp.stack(outs["fp"]),
            jnp.stack(outs["ks"]), jnp.stack(outs["vs"]), jnp.stack(outs["fs"]))
```
